```python
import math
import jax, jax.numpy as jnp
from jax import lax
import numpy as np

D_MODEL = 1024
BATCH = 2
SEQ = 8192
DEPTH = 1
DEC_BATCH = 128
DEC_SEQ = 8
PAST_LEN = 8192
PAGE_SIZE = 128

HEAD_DIM = 64
NSA_HEADS = 8
NSA_KV_HEADS = 2
CMP_LEN = 32
CMP_STRIDE = 16
CMP_HIDDEN = 128
SEL_LEN = 64
SEL_TOPK = 16
WINDOW = 512
DIFF_HEADS = 4
DIFF_DIM = 64
ROPE_THETA = 500000.0
ROT_FRAC = 4
MEM_HEADS = 4
MEM_HEAD_DIM = 128
PEER_HEADS = 8
PEER_KEYS = 128
PEER_EXPERTS = PEER_KEYS * PEER_KEYS
PEER_DK = 256
PEER_TOPK = 16
PEER_CHUNK = 256
Q_BLOCK = 128
RMS_EPS = 1e-6
NEG = -1e30
BIG = 1e9

NSA_Q = NSA_HEADS * HEAD_DIM
NSA_KV = NSA_KV_HEADS * HEAD_DIM
NSA_GATE = NSA_HEADS * 3
DIFF_W = DIFF_HEADS * 2 * DIFF_DIM
IN_SPLITS = (NSA_Q, NSA_KV, NSA_KV, NSA_KV, NSA_KV, NSA_KV, NSA_KV, NSA_GATE, DIFF_W, DIFF_W, DIFF_W)
IN_WIDTH = sum(IN_SPLITS)
MIX_WIDTH = NSA_Q + DIFF_W
MEM_W = MEM_HEADS * MEM_HEAD_DIM
MEM_LEN = 256

kernel_name = 'hymba_nsa_diffattn_peer_step'


def rms_norm(x, g):
    xf = x.astype(jnp.float32)
    y = xf * lax.rsqrt(jnp.mean(xf * xf, axis=-1, keepdims=True) + RMS_EPS)
    return (y * g.astype(jnp.float32)).astype(x.dtype)


def rope(x, pos):
    d = x.shape[-1]
    rot = d // ROT_FRAC
    half = rot // 2
    inv = jnp.power(jnp.float32(ROPE_THETA), -jnp.arange(half, dtype=jnp.float32) * 2.0 / rot)
    ang = pos.astype(jnp.float32)[:, None] * inv
    shp = (ang.shape[0],) + (1,) * (x.ndim - 3) + (half,)
    cos = jnp.cos(ang).reshape(shp)
    sin = jnp.sin(ang).reshape(shp)
    xf = x.astype(jnp.float32)
    x1, x2, rest = xf[..., :half], xf[..., half:rot], xf[..., rot:]
    out = jnp.concatenate([x1 * cos - x2 * sin, x2 * cos + x1 * sin, rest], axis=-1)
    return out.astype(x.dtype)


def masked_softmax(s, mask):
    s = jnp.where(mask, s.astype(jnp.float32), NEG)
    return jnp.where(mask, jax.nn.softmax(s, axis=-1), 0.0)


def compress_tokens(tok, pe, w1, w2):
    B, L, G, dh = tok.shape
    nc = (L - CMP_LEN) // CMP_STRIDE + 1
    idx = jnp.arange(nc)[:, None] * CMP_STRIDE + jnp.arange(CMP_LEN)[None, :]
    blk = tok[:, idx] + pe[:, None, :]
    blk = blk.transpose(0, 1, 3, 2, 4).reshape(B, nc, G, CMP_LEN * dh)
    return jax.nn.gelu(blk @ w1, approximate=False) @ w2


def to_blocks(t):
    B, L, G, dh = t.shape
    ns = -(-L // SEL_LEN)
    t = jnp.pad(t, ((0, 0), (0, ns * SEL_LEN - L), (0, 0), (0, 0)))
    return t.reshape(B, ns, SEL_LEN, G, dh).transpose(0, 3, 1, 2, 4)


def nsa_core(q, qr, q_pos, kc, vc, cmp_end, ksb, vsb, kw, vw, kw_pos, gates):
    B, Tq, H, dh = q.shape
    G = kc.shape[2]
    R = H // G
    scale = dh ** -0.5
    qg = q.reshape(B, Tq, G, R, dh)
    qrg = qr.reshape(B, Tq, G, R, dh)
    s = jnp.einsum('bqgrd,bngd->bqgrn', qg, kc) * scale
    m = (cmp_end[None, :] <= q_pos[:, None])[None, :, None, None, :]
    p_cmp = masked_softmax(s, m)
    o_cmp = jnp.einsum('bqgrn,bngd->bqgrd', p_cmp.astype(vc.dtype), vc)
    nc = kc.shape[1]
    ns = ksb.shape[2]
    sub = SEL_LEN // CMP_STRIDE
    n_sub = ns * sub
    imp = p_cmp.sum(axis=3)
    seg = sum(jnp.pad(imp, ((0, 0), (0, 0), (0, 0), (r, n_sub - nc - r)))
              for r in range(CMP_LEN // CMP_STRIDE))
    imp_blk = seg.reshape(B, Tq, G, ns, sub).sum(-1)
    blk = jnp.arange(ns)[None, :]
    cur = (q_pos // SEL_LEN)[:, None]
    forced = (blk == 0) | (blk == cur) | (blk == cur - 1)
    valid = blk <= cur
    score = jnp.where(forced[None, :, None, :], BIG, jnp.where(valid[None, :, None, :], imp_blk, -BIG))
    n_sel = min(SEL_TOPK, ns)
    _, idx = lax.top_k(score, n_sel)
    bi = jnp.arange(B)[:, None, None, None]
    gi = jnp.arange(G)[None, None, :, None]
    kg = ksb[bi, gi, idx].reshape(B, Tq, G, n_sel * SEL_LEN, dh)
    vg = vsb[bi, gi, idx].reshape(B, Tq, G, n_sel * SEL_LEN, dh)
    kpos = (idx[..., None] * SEL_LEN + jnp.arange(SEL_LEN)).reshape(B, Tq, G, n_sel * SEL_LEN)
    s = jnp.einsum('bqgrd,bqgmd->bqgrm', qrg, kg) * scale
    p = masked_softmax(s, (kpos <= q_pos[None, :, None, None])[:, :, :, None, :])
    o_slc = jnp.einsum('bqgrm,bqgmd->bqgrd', p.astype(vg.dtype), vg)
    s = jnp.einsum('bqgrd,bkgd->bqgrk', qrg, kw) * scale
    dist = q_pos[:, None] - kw_pos[None, :]
    m = ((dist >= 0) & (dist < WINDOW) & (kw_pos >= 0)[None, :])[None, :, None, None, :]
    p = masked_softmax(s, m)
    o_win = jnp.einsum('bqgrk,bkgd->bqgrd', p.astype(vw.dtype), vw)
    g = jax.nn.sigmoid(gates.astype(jnp.float32)).reshape(B, Tq, G, R, 3).astype(q.dtype)
    o = g[..., 0:1] * o_cmp + g[..., 1:2] * o_slc + g[..., 2:3] * o_win
    return o.reshape(B, Tq, H, dh)


def diff_core(qd, kd, vd, q_pos, k_pos, lam):
    s = jnp.einsum('bqhmd,bkhmd->bhmqk', qd, kd) * (DIFF_DIM ** -0.5)
    p = masked_softmax(s, k_pos[None, :] <= q_pos[:, None])
    a = p[:, :, 0] - lam * p[:, :, 1]
    return jnp.einsum('bhqk,bkhe->bqhe', a.astype(vd.dtype), vd)


def project_in(hn, w_in, pos):
    B, T, _ = hn.shape
    offs = [int(o) for o in np.cumsum(IN_SPLITS)[:-1]]
    q, kc, vc, ks, vs, kw, vw, gl, dq, dk, dv = jnp.split(hn @ w_in, offs, axis=-1)
    q = q.reshape(B, T, NSA_HEADS, HEAD_DIM)
    kv = lambda t: t.reshape(B, T, NSA_KV_HEADS, HEAD_DIM)
    dq = dq.reshape(B, T, DIFF_HEADS, 2, DIFF_DIM)
    dk = dk.reshape(B, T, DIFF_HEADS, 2, DIFF_DIM)
    return (q, rope(q, pos), kv(kc), kv(vc), rope(kv(ks), pos), kv(vs), rope(kv(kw), pos), kv(vw),
            gl.reshape(B, T, NSA_HEADS, 3), rope(dq, pos), rope(dk, pos),
            dv.reshape(B, T, DIFF_HEADS, 2 * DIFF_DIM))


def prompt_mixer(pp, lam, cmp_w):
    q, qr, kc_t, vc_t, ks, vs, kw, vw, gates, dq, dk, dv = pp
    B, T = q.shape[:2]
    pe_k, wk1, wk2, pe_v, wv1, wv2 = cmp_w
    kc = compress_tokens(kc_t, pe_k, wk1, wk2)
    vc = compress_tokens(vc_t, pe_v, wv1, wv2)
    cmp_end = jnp.arange(kc.shape[1]) * CMP_STRIDE + CMP_LEN - 1
    ksb, vsb = to_blocks(ks), to_blocks(vs)
    pad_w = ((0, 0), (WINDOW, 0), (0, 0), (0, 0))
    kwp, vwp = jnp.pad(kw, pad_w), jnp.pad(vw, pad_w)
    k_pos = jnp.arange(T)

    def block(i):
        q0 = i * Q_BLOCK
        sl = lambda t: lax.dynamic_slice_in_dim(t, q0, Q_BLOCK, axis=1)
        q_pos = q0 + jnp.arange(Q_BLOCK)
        kwb = lax.dynamic_slice_in_dim(kwp, q0, WINDOW + Q_BLOCK, axis=1)
        vwb = lax.dynamic_slice_in_dim(vwp, q0, WINDOW + Q_BLOCK, axis=1)
        kw_pos = q0 - WINDOW + jnp.arange(WINDOW + Q_BLOCK)
        o_n = nsa_core(sl(q), sl(qr), q_pos, kc, vc, cmp_end, ksb, vsb, kwb, vwb, kw_pos, sl(gates))
        o_d = diff_core(sl(dq), dk, dv, q_pos, k_pos, lam)
        return o_n, o_d

    o_n, o_d = lax.map(block, jnp.arange(T // Q_BLOCK))
    unblock = lambda t: jnp.moveaxis(t, 0, 1).reshape((B, T) + t.shape[3:])
    return unblock(o_n), unblock(o_d)


def sample_mixer(ps, page_table, c_cmp_k, c_cmp_v, c_slc_k, c_slc_v, c_diff_k, c_diff_v,
                 c_win_k, c_win_v, lam, cmp_w):
    q, qr, kc_t, vc_t, ks, vs, kw, vw, gates, dq, dk, dv = ps
    Bd, T = q.shape[:2]
    L = PAST_LEN + T
    q_pos = PAST_LEN + jnp.arange(T)
    k_pos = jnp.arange(L)
    w_buf = c_win_k.shape[1]
    kw_pos = PAST_LEN - w_buf + jnp.arange(w_buf + T)
    nc = (L - CMP_LEN) // CMP_STRIDE + 1
    cmp_end = jnp.arange(nc) * CMP_STRIDE + CMP_LEN - 1
    pe_k, wk1, wk2, pe_v, wv1, wv2 = cmp_w
    dk_rows = dk.reshape(Bd, T, DIFF_HEADS, 2 * DIFF_DIM)

    def one(args):
        pages, q1, qr1, kc1, vc1, ks1, vs1, kw1, vw1, g1, dq1, dk1, dv1, bk1, bv1 = args

        def full(cache, new):
            past = cache[pages].reshape((PAST_LEN,) + cache.shape[2:])
            return jnp.concatenate([past, new], axis=0)[None]

        kc = compress_tokens(full(c_cmp_k, kc1), pe_k, wk1, wk2)
        vc = compress_tokens(full(c_cmp_v, vc1), pe_v, wv1, wv2)
        ksb = to_blocks(full(c_slc_k, ks1))
        vsb = to_blocks(full(c_slc_v, vs1))
        kwb = jnp.concatenate([bk1, kw1], axis=0)[None]
        vwb = jnp.concatenate([bv1, vw1], axis=0)[None]
        o_n = nsa_core(q1[None], qr1[None], q_pos, kc, vc, cmp_end, ksb, vsb, kwb, vwb, kw_pos, g1[None])
        kd = full(c_diff_k, dk1).reshape(1, L, DIFF_HEADS, 2, DIFF_DIM)
        vd = full(c_diff_v, dv1)
        o_d = diff_core(dq1[None], kd, vd, q_pos, k_pos, lam)
        return o_n[0], o_d[0]

    return lax.map(one, (page_table, q, qr, kc_t, vc_t, ks, vs, kw, vw, gates, dq, dk_rows, dv,
                         c_win_k, c_win_v))


def mix_out(o_n, o_d, g_nsa, g_sub, w_out, lam_init):
    B, T = o_n.shape[:2]
    a = rms_norm(o_n, g_nsa.reshape(NSA_HEADS, HEAD_DIM))
    d = rms_norm(o_d, g_sub) * (1.0 - lam_init)
    return jnp.concatenate([a.reshape(B, T, NSA_Q), d.reshape(B, T, DIFF_W)], axis=-1) @ w_out


def mem_kv(mem, g_src, w_mk, w_mv):
    B, M, _ = mem.shape
    m = rms_norm(mem, g_src)
    return ((m @ w_mk).reshape(B, M, MEM_HEADS, MEM_HEAD_DIM),
            (m @ w_mv).reshape(B, M, MEM_HEADS, MEM_HEAD_DIM))


def mem_block(h, g, w_mq, mk, mv, w_mo):
    B, T, _ = h.shape
    q = (rms_norm(h, g) @ w_mq).reshape(B, T, MEM_HEADS, MEM_HEAD_DIM)
    s = jnp.einsum('bqhd,bmhd->bhqm', q, mk) * (MEM_HEAD_DIM ** -0.5)
    p = jax.nn.softmax(s.astype(jnp.float32), axis=-1)
    o = jnp.einsum('bhqm,bmhd->bqhd', p.astype(mv.dtype), mv)
    return h + o.reshape(B, T, MEM_W) @ w_mo


def peer_ffn(x, w_q, sub_keys, u, v):
    n = x.shape[0]
    pad = (-n) % PEER_CHUNK
    xp = jnp.pad(x, ((0, pad), (0, 0))).reshape(-1, PEER_CHUNK, D_MODEL)

    def chunk(xc):
        q = (xc @ w_q).reshape(PEER_CHUNK, PEER_HEADS, 2, PEER_DK // 2)
        s1 = jnp.einsum('thd,kd->thk', q[:, :, 0], sub_keys[0]).astype(jnp.float32)
        s2 = jnp.einsum('thd,kd->thk', q[:, :, 1], sub_keys[1]).astype(jnp.float32)
        v1, i1 = lax.top_k(s1, PEER_TOPK)
        v2, i2 = lax.top_k(s2, PEER_TOPK)
        cand = (v1[..., :, None] + v2[..., None, :]).reshape(PEER_CHUNK, PEER_HEADS, PEER_TOPK * PEER_TOPK)
        cidx = (i1[..., :, None] * PEER_KEYS + i2[..., None, :]).reshape(PEER_CHUNK, PEER_HEADS, PEER_TOPK * PEER_TOPK)
        sc, sel = lax.top_k(cand, PEER_TOPK)
        eidx = jnp.take_along_axis(cidx, sel, axis=-1)
        g = jax.nn.softmax(sc, axis=-1)
        ue = u[eidx]
        ve = v[eidx]
        act = jax.nn.gelu(jnp.einsum('td,thkd->thk', xc, ue).astype(jnp.float32), approximate=False)
        return jnp.einsum('thk,thkd->td', (g * act).astype(xc.dtype), ve)

    return lax.map(chunk, xp).reshape(-1, D_MODEL)[:n]


def ffn_block(h, g, w_q, sub_keys, u, v):
    B, T, D = h.shape
    return h + peer_ffn(rms_norm(h, g).reshape(B * T, D), w_q, sub_keys, u, v).reshape(B, T, D)


def setup_inputs(seed: int = 0) -> dict:
    key = jax.random.key(seed)
    keys = iter(jax.random.split(key, 64))
    f32 = jnp.float32
    L = DEPTH
    rn = lambda shape: jax.random.normal(next(keys), shape, f32)
    w = lambda shape, fan_in: rn(shape) * (fan_in ** -0.5)
    gain = lambda shape: 1.0 + 0.01 * rn(shape)
    n_pages = PAST_LEN // PAGE_SIZE
    n_phys = (5 * DEC_BATCH * n_pages) // 4
    w_buf = min(WINDOW, PAST_LEN)
    page_table = jax.random.permutation(next(keys), n_phys)[:DEC_BATCH * n_pages]
    page_table = page_table.reshape(DEC_BATCH, n_pages).astype(jnp.int32)
    nsa_page = (L, n_phys, PAGE_SIZE, NSA_KV_HEADS, HEAD_DIM)
    diff_page = (L, n_phys, PAGE_SIZE, DIFF_HEADS, 2 * DIFF_DIM)
    return {
        'x_prompt': rn((BATCH, SEQ, D_MODEL)),
        'x_sample': rn((DEC_BATCH, DEC_SEQ, D_MODEL)),
        'cache_cmp_k': rn(nsa_page),
        'cache_cmp_v': rn(nsa_page),
        'cache_slc_k': rn(nsa_page),
        'cache_slc_v': rn(nsa_page),
        'cache_diff_k': rn(diff_page),
        'cache_diff_v': rn(diff_page),
        'cache_win_k': rn((L, DEC_BATCH, w_buf, NSA_KV_HEADS, HEAD_DIM)),
        'cache_win_v': rn((L, DEC_BATCH, w_buf, NSA_KV_HEADS, HEAD_DIM)),
        'cache_mem_k': rn((L, DEC_BATCH, MEM_LEN, MEM_HEADS, MEM_HEAD_DIM)),
        'cache_mem_v': rn((L, DEC_BATCH, MEM_LEN, MEM_HEADS, MEM_HEAD_DIM)),
        'page_table': page_table,
        'mem_prompt': rn((BATCH, MEM_LEN, D_MODEL)),
        'norm_mix': gain((L, D_MODEL)),
        'w_in': w((L, D_MODEL, IN_WIDTH), D_MODEL),
        'cmp_pe_k': 0.1 * rn((L, CMP_LEN, HEAD_DIM)),
        'cmp_pe_v': 0.1 * rn((L, CMP_LEN, HEAD_DIM)),
        'cmp_k_w1': w((L, CMP_LEN * HEAD_DIM, CMP_HIDDEN), CMP_LEN * HEAD_DIM),
        'cmp_k_w2': w((L, CMP_HIDDEN, HEAD_DIM), CMP_HIDDEN),
        'cmp_v_w1': w((L, CMP_LEN * HEAD_DIM, CMP_HIDDEN), CMP_LEN * HEAD_DIM),
        'cmp_v_w2': w((L, CMP_HIDDEN, HEAD_DIM), CMP_HIDDEN),
        'nsa_out_norm': gain((L, NSA_Q)),
        'diff_lq1': 0.1 * rn((L, DIFF_DIM)),
        'diff_lk1': 0.1 * rn((L, DIFF_DIM)),
        'diff_lq2': 0.1 * rn((L, DIFF_DIM)),
        'diff_lk2': 0.1 * rn((L, DIFF_DIM)),
        'diff_subln': gain((L, 2 * DIFF_DIM)),
        'w_out': w((L, MIX_WIDTH, D_MODEL), MIX_WIDTH),
        'norm_mem_q': gain((L, D_MODEL)),
        'norm_mem_src': gain((L, D_MODEL)),
        'w_mq': w((L, D_MODEL, MEM_W), D_MODEL),
        'w_mk': w((L, D_MODEL, MEM_W), D_MODEL),
        'w_mv': w((L, D_MODEL, MEM_W), D_MODEL),
        'w_mo': w((L, MEM_W, D_MODEL), MEM_W),
        'norm_ffn': gain((L, D_MODEL)),
        'peer_wq': w((L, D_MODEL, PEER_HEADS * PEER_DK), D_MODEL),
        'peer_keys': w((L, 2, PEER_KEYS, PEER_DK // 2), PEER_DK // 2),
        'peer_u': w((L, PEER_EXPERTS, D_MODEL), D_MODEL),
        'peer_v': w((L, PEER_EXPERTS, D_MODEL), PEER_HEADS),
        'norm_final': gain((D_MODEL,)),
    }


def reference(x_prompt, x_sample, cache_cmp_k, cache_cmp_v, cache_slc_k, cache_slc_v,
              cache_diff_k, cache_diff_v, cache_win_k, cache_win_v, cache_mem_k, cache_mem_v,
              page_table, mem_prompt, norm_mix, w_in, cmp_pe_k, cmp_pe_v, cmp_k_w1, cmp_k_w2,
              cmp_v_w1, cmp_v_w2, nsa_out_norm, diff_lq1, diff_lk1, diff_lq2, diff_lk2, diff_subln,
              w_out, norm_mem_q, norm_mem_src, w_mq, w_mk, w_mv, w_mo, norm_ffn, peer_wq, peer_keys,
              peer_u, peer_v, norm_final):
    f32 = jnp.float32
    hp, hs = x_prompt, x_sample
    pos_p = jnp.arange(x_prompt.shape[1])
    pos_s = PAST_LEN + jnp.arange(x_sample.shape[1])
    names = ['p_cmp_k', 'p_cmp_v', 'p_slc_k', 'p_slc_v', 'p_win_k', 'p_win_v', 'p_diff_k', 'p_diff_v',
             'p_mem_k', 'p_mem_v', 's_cmp_k', 's_cmp_v', 's_slc_k', 's_slc_v', 's_win_k', 's_win_v',
             's_diff_k', 's_diff_v']
    st = {n: [] for n in names}
    for l in range(DEPTH):
        lam_init = 0.8 - 0.6 * math.exp(-0.3 * l)
        lam = (jnp.exp(jnp.sum(diff_lq1[l].astype(f32) * diff_lk1[l].astype(f32)))
               - jnp.exp(jnp.sum(diff_lq2[l].astype(f32) * diff_lk2[l].astype(f32))) + lam_init)
        cmp_w = (cmp_pe_k[l], cmp_k_w1[l], cmp_k_w2[l], cmp_pe_v[l], cmp_v_w1[l], cmp_v_w2[l])

        pp = project_in(rms_norm(hp, norm_mix[l]), w_in[l], pos_p)
        o_n, o_d = prompt_mixer(pp, lam, cmp_w)
        hp = hp + mix_out(o_n, o_d, nsa_out_norm[l], diff_subln[l], w_out[l], lam_init)
        mk, mv = mem_kv(mem_prompt, norm_mem_src[l], w_mk[l], w_mv[l])
        hp = mem_block(hp, norm_mem_q[l], w_mq[l], mk, mv, w_mo[l])
        hp = ffn_block(hp, norm_ffn[l], peer_wq[l], peer_keys[l], peer_u[l], peer_v[l])
        Bp, Tp = x_prompt.shape[:2]
        w_p = min(WINDOW, Tp)
        st['p_cmp_k'].append(pp[2])
        st['p_cmp_v'].append(pp[3])
        st['p_slc_k'].append(pp[4])
        st['p_slc_v'].append(pp[5])
        st['p_win_k'].append(pp[6][:, Tp - w_p:])
        st['p_win_v'].append(pp[7][:, Tp - w_p:])
        st['p_diff_k'].append(pp[10].reshape(Bp, Tp, DIFF_HEADS, 2 * DIFF_DIM))
        st['p_diff_v'].append(pp[11])
        st['p_mem_k'].append(mk)
        st['p_mem_v'].append(mv)

        ps = project_in(rms_norm(hs, norm_mix[l]), w_in[l], pos_s)
        o_n, o_d = sample_mixer(ps, page_table, cache_cmp_k[l], cache_cmp_v[l], cache_slc_k[l],
                                cache_slc_v[l], cache_diff_k[l], cache_diff_v[l], cache_win_k[l],
                                cache_win_v[l], lam, cmp_w)
        hs = hs + mix_out(o_n, o_d, nsa_out_norm[l], diff_subln[l], w_out[l], lam_init)
        hs = mem_block(hs, norm_mem_q[l], w_mq[l], cache_mem_k[l], cache_mem_v[l], w_mo[l])
        hs = ffn_block(hs, norm_ffn[l], peer_wq[l], peer_keys[l], peer_u[l], peer_v[l])
        Bs, Ts = x_sample.shape[:2]
        w_s = cache_win_k.shape[2]
        st['s_cmp_k'].append(ps[2])
        st['s_cmp_v'].append(ps[3])
        st['s_slc_k'].append(ps[4])
        st['s_slc_v'].append(ps[5])
        st['s_win_k'].append(jnp.concatenate([cache_win_k[l], ps[6]], axis=1)[:, Ts:])
        st['s_win_v'].append(jnp.concatenate([cache_win_v[l], ps[7]], axis=1)[:, Ts:])
        st['s_diff_k'].append(ps[10].reshape(Bs, Ts, DIFF_HEADS, 2 * DIFF_DIM))
        st['s_diff_v'].append(ps[11])

    y_prompt = rms_norm(hp, norm_final)
    y_sample = rms_norm(hs, norm_final)
    return (y_prompt, y_sample,
            jnp.stack(st['p_cmp_k']), jnp.stack(st['p_cmp_v']), jnp.stack(st['p_slc_k']),
            jnp.stack(st['p_slc_v']), jnp.stack(st['p_win_k']), jnp.stack(st['p_win_v']),
            jnp.stack(st['p_diff_k']), jnp.stack(st['p_diff_v']), jnp.stack(st['p_mem_k']),
            jnp.stack(st['p_mem_v']),
            jnp.stack(st['s_cmp_k']), jnp.stack(st['s_cmp_v']), jnp.stack(st['s_slc_k']),
            jnp.stack(st['s_slc_v']), jnp.stack(st['s_win_k']), jnp.stack(st['s_win_v']),
            jnp.stack(st['s_diff_k']), jnp.stack(st['s_diff_v']))
```

```python
import math
import jax, jax.numpy as jnp
from jax import lax
import numpy as np
from jax.experimental import pallas as pl
from jax.experimental.pallas import tpu as pltpu

D_MODEL = 1024
BATCH = 2
SEQ = 8192
DEPTH = 1
DEC_BATCH = 128
DEC_SEQ = 8
PAST_LEN = 8192
PAGE_SIZE = 128
HEAD_DIM = 64
NSA_HEADS = 8
NSA_KV_HEADS = 2
CMP_LEN = 32
CMP_STRIDE = 16
CMP_HIDDEN = 128
SEL_LEN = 64
SEL_TOPK = 16
WINDOW = 512
DIFF_HEADS = 4
DIFF_DIM = 64
ROPE_THETA = 500000.0
ROT_FRAC = 4
MEM_HEADS = 4
MEM_HEAD_DIM = 128
PEER_HEADS = 8
PEER_KEYS = 128
PEER_EXPERTS = PEER_KEYS * PEER_KEYS
PEER_DK = 256
PEER_TOPK = 16
PEER_CHUNK = 256
Q_BLOCK = 128
RMS_EPS = 1e-6
NEG = -1e30
BIG = 1e9

NSA_Q = NSA_HEADS * HEAD_DIM
NSA_KV = NSA_KV_HEADS * HEAD_DIM
NSA_GATE = NSA_HEADS * 3
DIFF_W = DIFF_HEADS * 2 * DIFF_DIM
IN_SPLITS = (NSA_Q, NSA_KV, NSA_KV, NSA_KV, NSA_KV, NSA_KV, NSA_KV, NSA_GATE, DIFF_W, DIFF_W, DIFF_W)
IN_WIDTH = sum(IN_SPLITS)
MIX_WIDTH = NSA_Q + DIFF_W
MEM_W = MEM_HEADS * MEM_HEAD_DIM
MEM_LEN = 256


def rms_norm(x, g):
    xf = x.astype(jnp.float32)
    y = xf * lax.rsqrt(jnp.mean(xf * xf, axis=-1, keepdims=True) + RMS_EPS)
    return (y * g.astype(jnp.float32)).astype(x.dtype)


def rope(x, pos):
    d = x.shape[-1]
    rot = d // ROT_FRAC
    half = rot // 2
    inv = jnp.power(jnp.float32(ROPE_THETA), -jnp.arange(half, dtype=jnp.float32) * 2.0 / rot)
    ang = pos.astype(jnp.float32)[:, None] * inv
    shp = (ang.shape[0],) + (1,) * (x.ndim - 3) + (half,)
    cos = jnp.cos(ang).reshape(shp)
    sin = jnp.sin(ang).reshape(shp)
    xf = x.astype(jnp.float32)
    x1, x2, rest = xf[..., :half], xf[..., half:rot], xf[..., rot:]
    out = jnp.concatenate([x1 * cos - x2 * sin, x2 * cos + x1 * sin, rest], axis=-1)
    return out.astype(x.dtype)


def masked_softmax(s, mask):
    s = jnp.where(mask, s.astype(jnp.float32), NEG)
    return jnp.where(mask, jax.nn.softmax(s, axis=-1), 0.0)


def compress_tokens(tok, pe, w1, w2):
    B, L, G, dh = tok.shape
    nc = (L - CMP_LEN) // CMP_STRIDE + 1
    idx = jnp.arange(nc)[:, None] * CMP_STRIDE + jnp.arange(CMP_LEN)[None, :]
    blk = tok[:, idx] + pe[:, None, :]
    blk = blk.transpose(0, 1, 3, 2, 4).reshape(B, nc, G, CMP_LEN * dh)
    return jax.nn.gelu(blk @ w1, approximate=False) @ w2


def to_blocks(t):
    B, L, G, dh = t.shape
    ns = -(-L // SEL_LEN)
    t = jnp.pad(t, ((0, 0), (0, ns * SEL_LEN - L), (0, 0), (0, 0)))
    return t.reshape(B, ns, SEL_LEN, G, dh).transpose(0, 3, 1, 2, 4)


def nsa_core(q, qr, q_pos, kc, vc, cmp_end, ksb, vsb, kw, vw, kw_pos, gates):
    B, Tq, H, dh = q.shape
    G = kc.shape[2]
    R = H // G
    scale = dh ** -0.5
    qg = q.reshape(B, Tq, G, R, dh)
    qrg = qr.reshape(B, Tq, G, R, dh)
    s = jnp.einsum('bqgrd,bngd->bqgrn', qg, kc) * scale
    m = (cmp_end[None, :] <= q_pos[:, None])[None, :, None, None, :]
    p_cmp = masked_softmax(s, m)
    o_cmp = jnp.einsum('bqgrn,bngd->bqgrd', p_cmp.astype(vc.dtype), vc)
    nc = kc.shape[1]
    ns = ksb.shape[2]
    sub = SEL_LEN // CMP_STRIDE
    n_sub = ns * sub
    imp = p_cmp.sum(axis=3)
    seg = sum(jnp.pad(imp, ((0, 0), (0, 0), (0, 0), (r, n_sub - nc - r)))
              for r in range(CMP_LEN // CMP_STRIDE))
    imp_blk = seg.reshape(B, Tq, G, ns, sub).sum(-1)
    blk = jnp.arange(ns)[None, :]
    cur = (q_pos // SEL_LEN)[:, None]
    forced = (blk == 0) | (blk == cur) | (blk == cur - 1)
    valid = blk <= cur
    score = jnp.where(forced[None, :, None, :], BIG, jnp.where(valid[None, :, None, :], imp_blk, -BIG))
    n_sel = min(SEL_TOPK, ns)
    _, idx = lax.top_k(score, n_sel)
    bi = jnp.arange(B)[:, None, None, None]
    gi = jnp.arange(G)[None, None, :, None]
    kg = ksb[bi, gi, idx].reshape(B, Tq, G, n_sel * SEL_LEN, dh)
    vg = vsb[bi, gi, idx].reshape(B, Tq, G, n_sel * SEL_LEN, dh)
    kpos = (idx[..., None] * SEL_LEN + jnp.arange(SEL_LEN)).reshape(B, Tq, G, n_sel * SEL_LEN)
    s = jnp.einsum('bqgrd,bqgmd->bqgrm', qrg, kg) * scale
    p = masked_softmax(s, (kpos <= q_pos[None, :, None, None])[:, :, :, None, :])
    o_slc = jnp.einsum('bqgrm,bqgmd->bqgrd', p.astype(vg.dtype), vg)
    s = jnp.einsum('bqgrd,bkgd->bqgrk', qrg, kw) * scale
    dist = q_pos[:, None] - kw_pos[None, :]
    m = ((dist >= 0) & (dist < WINDOW) & (kw_pos >= 0)[None, :])[None, :, None, None, :]
    p = masked_softmax(s, m)
    o_win = jnp.einsum('bqgrk,bkgd->bqgrd', p.astype(vw.dtype), vw)
    g = jax.nn.sigmoid(gates.astype(jnp.float32)).reshape(B, Tq, G, R, 3).astype(q.dtype)
    o = g[..., 0:1] * o_cmp + g[..., 1:2] * o_slc + g[..., 2:3] * o_win
    return o.reshape(B, Tq, H, dh)


def diff_core(qd, kd, vd, q_pos, k_pos, lam):
    s = jnp.einsum('bqhmd,bkhmd->bhmqk', qd, kd) * (DIFF_DIM ** -0.5)
    p = masked_softmax(s, k_pos[None, :] <= q_pos[:, None])
    a = p[:, :, 0] - lam * p[:, :, 1]
    return jnp.einsum('bhqk,bkhe->bqhe', a.astype(vd.dtype), vd)


def project_in(hn, w_in, pos):
    B, T, _ = hn.shape
    offs = [int(o) for o in np.cumsum(IN_SPLITS)[:-1]]
    q, kc, vc, ks, vs, kw, vw, gl, dq, dk, dv = jnp.split(hn @ w_in, offs, axis=-1)
    q = q.reshape(B, T, NSA_HEADS, HEAD_DIM)
    kv = lambda t: t.reshape(B, T, NSA_KV_HEADS, HEAD_DIM)
    dq = dq.reshape(B, T, DIFF_HEADS, 2, DIFF_DIM)
    dk = dk.reshape(B, T, DIFF_HEADS, 2, DIFF_DIM)
    return (q, rope(q, pos), kv(kc), kv(vc), rope(kv(ks), pos), kv(vs), rope(kv(kw), pos), kv(vw),
            gl.reshape(B, T, NSA_HEADS, 3), rope(dq, pos), rope(dk, pos),
            dv.reshape(B, T, DIFF_HEADS, 2 * DIFF_DIM))


BF16 = jnp.bfloat16
F32 = jnp.float32
GQA_REP = NSA_HEADS // NSA_KV_HEADS
N_SEL_BLOCKS_PER_TILE = None


def _online_update(s, mask, v, m_ref, l_ref, acc_ref):
    m_old = m_ref[...]
    if mask is not None:
        s = jnp.where(mask, s, NEG)
    m_new = jnp.maximum(m_old, jnp.max(s, axis=-1, keepdims=True))
    p = jnp.exp(s - m_new)
    if mask is not None:
        p = jnp.where(mask, p, 0.0)
    alpha = jnp.exp(m_old - m_new)
    l_ref[...] = alpha * l_ref[...] + jnp.sum(p, axis=-1, keepdims=True)
    lead = p.shape[:-2]
    p2 = p.astype(BF16).reshape((-1, p.shape[-1]))
    pv = jnp.dot(p2, v, preferred_element_type=F32).reshape(lead + (p.shape[-2], v.shape[-1]))
    acc_ref[...] = alpha * acc_ref[...] + pv
    m_ref[...] = m_new


def _diff_body(lam_ref, q_ref, k_ref, v_ref, o_ref, m_sc, l_sc, acc_sc, *, tq):
    i = pl.program_id(2)
    j = pl.program_id(3)

    @pl.when(j == 0)
    def _():
        m_sc[...] = jnp.full(m_sc.shape, NEG, F32)
        l_sc[...] = jnp.zeros(l_sc.shape, F32)
        acc_sc[...] = jnp.zeros(acc_sc.shape, F32)

    def step(masked):
        v = v_ref[0]
        if masked:
            row = lax.broadcasted_iota(jnp.int32, (tq, tq), 0)
            col = lax.broadcasted_iota(jnp.int32, (tq, tq), 1)
            mask = col <= row
        else:
            mask = None
        for mp in range(2):
            s = lax.dot_general(q_ref[0, 0, mp], k_ref[0, 0, mp], (((1,), (1,)), ((), ())),
                                preferred_element_type=F32)
            _online_update(s, mask, v, m_sc.at[mp], l_sc.at[mp], acc_sc.at[mp])

    @pl.when(j < i)
    def _():
        step(False)

    @pl.when(j == i)
    def _():
        step(True)
        lam = lam_ref[0]
        o_ref[0] = acc_sc[0] / l_sc[0] - lam * (acc_sc[1] / l_sc[1])


def diff_attention_causal(dq, dk, dv, lam, *, tq=512):
    B, H, _, T, dd = dq.shape
    nq = T // tq
    body = lambda *a: _diff_body(*a, tq=tq)
    return pl.pallas_call(
        body,
        grid=(B, H, nq, nq),
        in_specs=[
            pl.BlockSpec(memory_space=pltpu.SMEM),
            pl.BlockSpec((1, 1, 2, tq, dd), lambda b, h, i, j: (b, h, 0, i, 0)),
            pl.BlockSpec((1, 1, 2, tq, dd), lambda b, h, i, j: (b, h, 0, jnp.minimum(j, i), 0)),
            pl.BlockSpec((1, tq, 2 * dd), lambda b, h, i, j: (b, jnp.minimum(j, i), h)),
        ],
        out_specs=pl.BlockSpec((1, tq, 2 * dd), lambda b, h, i, j: (b, i, h)),
        out_shape=jax.ShapeDtypeStruct((B, T, H * 2 * dd), F32),
        scratch_shapes=[pltpu.VMEM((2, tq, 1), F32), pltpu.VMEM((2, tq, 1), F32),
                        pltpu.VMEM((2, tq, 2 * dd), F32)],
        compiler_params=pltpu.CompilerParams(
            dimension_semantics=("parallel", "parallel", "parallel", "arbitrary")),
        name="diff_attention",
    )(lam.reshape(1).astype(F32), dq, dk, dv)


def _cmp_select_body(q_ref, kc_ref, vc_ref, segmat_ref, o_ref, sel_ref, *, tq, n_sel):
    i = pl.program_id(2)
    R = q_ref.shape[2]
    ncp = kc_ref.shape[2]
    ns = sel_ref.shape[3]
    q0 = i * tq
    q_pos = q0 + lax.broadcasted_iota(jnp.int32, (tq, 1), 0)
    cmp_end = lax.broadcasted_iota(jnp.int32, (1, ncp), 1) * CMP_STRIDE + (CMP_LEN - 1)
    mask = cmp_end <= q_pos
    kc = kc_ref[0, 0]
    vc = vc_ref[0, 0]
    imp = jnp.zeros((tq, ncp), F32)
    for r in range(R):
        s = lax.dot_general(q_ref[0, 0, r], kc, (((1,), (1,)), ((), ())), preferred_element_type=F32)
        s = jnp.where(mask, s, NEG)
        m = jnp.max(s, axis=-1, keepdims=True)
        p = jnp.where(mask, jnp.exp(s - m), 0.0)
        l = jnp.sum(p, axis=-1, keepdims=True)
        p = p * jnp.where(l > 0.0, 1.0 / l, 0.0)
        o_ref[0, 0, r] = jnp.dot(p.astype(BF16), vc, preferred_element_type=F32)
        imp = imp + p
    imp_blk = jnp.dot(imp, segmat_ref[...], preferred_element_type=F32,
                      precision=lax.Precision.HIGHEST)
    blk = lax.broadcasted_iota(jnp.int32, (1, ns), 1)
    cur = q_pos // SEL_LEN
    forced = (blk == 0) | (blk == cur) | (blk == cur - 1)
    valid = blk <= cur
    work = jnp.where(forced, BIG, jnp.where(valid, imp_blk, -BIG))
    sel = jnp.zeros((tq, ns), F32)
    for _ in range(n_sel):
        mx = jnp.max(work, axis=-1, keepdims=True)
        first = jnp.min(jnp.where(work == mx, blk, ns), axis=-1, keepdims=True)
        pick = blk == first
        sel = jnp.where(pick, 1.0, sel)
        work = jnp.where(pick, -3.0e38, work)
    sel_ref[0, 0] = sel.astype(sel_ref.dtype)


def _segment_matrix(ncp, ns):
    sub = SEL_LEN // CMP_STRIDE
    n = np.arange(ncp)[:, None]
    j = np.arange(ns)[None, :]
    m = ((n >= sub * j) & (n < sub * j + sub)).astype(np.float32)
    for r in range(1, CMP_LEN // CMP_STRIDE):
        m = m + ((n + r >= sub * j) & (n + r < sub * j + sub)).astype(np.float32)
    return jnp.asarray(m, F32)


def nsa_cmp_select(q, kc, vc, ns, *, tq=256):
    B, G, R, T, dh = q.shape
    ncp = kc.shape[2]
    n_sel = min(SEL_TOPK, ns)
    body = lambda *a: _cmp_select_body(*a, tq=tq, n_sel=n_sel)
    return pl.pallas_call(
        body,
        grid=(B, G, T // tq),
        in_specs=[
            pl.BlockSpec((1, 1, R, tq, dh), lambda b, g, i: (b, g, 0, i, 0)),
            pl.BlockSpec((1, 1, ncp, dh), lambda b, g, i: (b, g, 0, 0)),
            pl.BlockSpec((1, 1, ncp, dh), lambda b, g, i: (b, g, 0, 0)),
            pl.BlockSpec((ncp, ns), lambda b, g, i: (0, 0)),
        ],
        out_specs=[
            pl.BlockSpec((1, 1, R, tq, dh), lambda b, g, i: (b, g, 0, i, 0)),
            pl.BlockSpec((1, 1, tq, ns), lambda b, g, i: (b, g, i, 0)),
        ],
        out_shape=[jax.ShapeDtypeStruct((B, G, R, T, dh), F32),
                   jax.ShapeDtypeStruct((B, G, T, ns), BF16)],
        compiler_params=pltpu.CompilerParams(dimension_semantics=("parallel", "parallel", "parallel")),
        name="nsa_cmp_select",
    )(q, kc, vc, _segment_matrix(ncp, ns))


def _slc_win_body(q_ref, ks_ref, vs_ref, kw_ref, vw_ref, sel_ref, exp_ref, os_ref, ow_ref,
                  ms_sc, ls_sc, as_sc, mw_sc, lw_sc, aw_sc, *, tq, tk):
    i = pl.program_id(2)
    j = pl.program_id(3)
    R = q_ref.shape[2]
    dh = q_ref.shape[4]
    q0 = i * tq
    last = (q0 + tq - 1) // tk
    first_win = jnp.maximum((q0 - (WINDOW - 1)) // tk, 0)

    @pl.when(j == 0)
    def _():
        for m_sc, l_sc, a_sc in ((ms_sc, ls_sc, as_sc), (mw_sc, lw_sc, aw_sc)):
            m_sc[...] = jnp.full(m_sc.shape, NEG, F32)
            l_sc[...] = jnp.zeros(l_sc.shape, F32)
            a_sc[...] = jnp.zeros(a_sc.shape, F32)

    def positions():
        q_pos = q0 + lax.broadcasted_iota(jnp.int32, (tq, tk), 0)
        k_pos = j * tk + lax.broadcasted_iota(jnp.int32, (tq, tk), 1)
        return q_pos, k_pos

    q = q_ref[0, 0].reshape(R * tq, dh)

    def scores(k_ref):
        s = lax.dot_general(q, k_ref[0, 0], (((1,), (1,)), ((), ())), preferred_element_type=F32)
        return s.reshape(R, tq, tk)

    def slc_step(diagonal):
        picked = jnp.dot(sel_ref[0, 0], exp_ref[...], preferred_element_type=F32) > 0.5
        if diagonal:
            q_pos, k_pos = positions()
            picked = picked & (k_pos <= q_pos)
        _online_update(scores(ks_ref), picked[None], vs_ref[0, 0], ms_sc, ls_sc, as_sc)

    def win_step():
        q_pos, k_pos = positions()
        dist = q_pos - k_pos
        mask = (dist >= 0) & (dist < WINDOW)
        _online_update(scores(kw_ref), mask[None], vw_ref[0, 0], mw_sc, lw_sc, aw_sc)

    @pl.when(j < last)
    def _():
        slc_step(False)

    @pl.when((j >= first_win) & (j <= last))
    def _():
        win_step()

    @pl.when(j == last)
    def _():
        slc_step(True)
        os_ref[0, 0] = as_sc[...] / ls_sc[...]
        ow_ref[0, 0] = aw_sc[...] / lw_sc[...]


def nsa_slc_win(qr, ks, vs, kw, vw, sel, *, tq=256, tk=512):
    B, G, R, T, dh = qr.shape
    ns = sel.shape[3]
    nk = T // tk
    expand = (jnp.arange(T)[None, :] // SEL_LEN == jnp.arange(ns)[:, None]).astype(BF16)
    last = lambda i: (i * tq + tq - 1) // tk
    kidx = lambda b, g, i, j: (b, g, jnp.minimum(j, last(i)), 0)
    widx = lambda b, g, i, j: (b, g, jnp.clip(j, jnp.maximum((i * tq - (WINDOW - 1)) // tk, 0), last(i)), 0)
    body = lambda *a: _slc_win_body(*a, tq=tq, tk=tk)
    o_spec = pl.BlockSpec((1, 1, R, tq, dh), lambda b, g, i, j: (b, g, 0, i, 0))
    return pl.pallas_call(
        body,
        grid=(B, G, T // tq, nk),
        in_specs=[
            pl.BlockSpec((1, 1, R, tq, dh), lambda b, g, i, j: (b, g, 0, i, 0)),
            pl.BlockSpec((1, 1, tk, dh), kidx),
            pl.BlockSpec((1, 1, tk, dh), kidx),
            pl.BlockSpec((1, 1, tk, dh), widx),
            pl.BlockSpec((1, 1, tk, dh), widx),
            pl.BlockSpec((1, 1, tq, ns), lambda b, g, i, j: (b, g, i, 0)),
            pl.BlockSpec((ns, tk), lambda b, g, i, j: (0, jnp.minimum(j, last(i)))),
        ],
        out_specs=[o_spec, o_spec],
        out_shape=[jax.ShapeDtypeStruct((B, G, R, T, dh), F32)] * 2,
        scratch_shapes=[pltpu.VMEM((R, tq, 1), F32), pltpu.VMEM((R, tq, 1), F32), pltpu.VMEM((R, tq, dh), F32),
                        pltpu.VMEM((R, tq, 1), F32), pltpu.VMEM((R, tq, 1), F32), pltpu.VMEM((R, tq, dh), F32)],
        compiler_params=pltpu.CompilerParams(
            dimension_semantics=("parallel", "parallel", "parallel", "arbitrary")),
        name="nsa_slc_win",
    )(qr, ks, vs, kw, vw, sel, expand)


def _heads_major(t, scale=None):
    B, T, H, dh = t.shape
    if scale is not None:
        t = t * scale
    return t.astype(BF16).reshape(B, T, NSA_KV_HEADS, H // NSA_KV_HEADS, dh).transpose(0, 2, 3, 1, 4)


def _groups_major(t):
    return t.astype(BF16).transpose(0, 2, 1, 3)


def prompt_mixer(pp, lam, cmp_w):
    q, qr, kc_t, vc_t, ks, vs, kw, vw, gates, dq, dk, dv = pp
    B, T = q.shape[:2]
    pe_k, wk1, wk2, pe_v, wv1, wv2 = cmp_w
    kc = compress_tokens(kc_t, pe_k, wk1, wk2)
    vc = compress_tokens(vc_t, pe_v, wv1, wv2)
    nc = kc.shape[1]
    ns = -(-T // SEL_LEN)
    ncp = ns * (SEL_LEN // CMP_STRIDE)
    pad_c = lambda t: _groups_major(jnp.pad(t, ((0, 0), (0, ncp - nc), (0, 0), (0, 0))))
    scale = HEAD_DIM ** -0.5
    o_cmp, sel = nsa_cmp_select(_heads_major(q, scale), pad_c(kc), pad_c(vc), ns)
    o_slc, o_win = nsa_slc_win(_heads_major(qr, scale), _groups_major(ks), _groups_major(vs),
                               _groups_major(kw), _groups_major(vw), sel)
    back = lambda t: t.transpose(0, 3, 1, 2, 4)
    g = jax.nn.sigmoid(gates.astype(F32)).reshape(B, T, NSA_KV_HEADS, GQA_REP, 3)
    o_n = (g[..., 0:1] * back(o_cmp) + g[..., 1:2] * back(o_slc) + g[..., 2:3] * back(o_win))
    o_n = o_n.reshape(B, T, NSA_HEADS, HEAD_DIM)

    dscale = DIFF_DIM ** -0.5
    dqh = (dq * dscale).astype(BF16).transpose(0, 2, 3, 1, 4)
    dkh = dk.astype(BF16).transpose(0, 2, 3, 1, 4)
    o_d = diff_attention_causal(dqh, dkh, dv.reshape(B, T, DIFF_W).astype(BF16), lam)
    return o_n, o_d.reshape(B, T, DIFF_HEADS, 2 * DIFF_DIM)


def sample_mixer(ps, page_table, c_cmp_k, c_cmp_v, c_slc_k, c_slc_v, c_diff_k, c_diff_v,
                 c_win_k, c_win_v, lam, cmp_w):
    q, qr, kc_t, vc_t, ks, vs, kw, vw, gates, dq, dk, dv = ps
    Bd, T = q.shape[:2]
    L = PAST_LEN + T
    q_pos = PAST_LEN + jnp.arange(T)
    k_pos = jnp.arange(L)
    w_buf = c_win_k.shape[1]
    kw_pos = PAST_LEN - w_buf + jnp.arange(w_buf + T)
    nc = (L - CMP_LEN) // CMP_STRIDE + 1
    cmp_end = jnp.arange(nc) * CMP_STRIDE + CMP_LEN - 1
    pe_k, wk1, wk2, pe_v, wv1, wv2 = cmp_w
    dk_rows = dk.reshape(Bd, T, DIFF_HEADS, 2 * DIFF_DIM)

    def one(args):
        pages, q1, qr1, kc1, vc1, ks1, vs1, kw1, vw1, g1, dq1, dk1, dv1, bk1, bv1 = args

        def full(cache, new):
            past = cache[pages].reshape((PAST_LEN,) + cache.shape[2:])
            return jnp.concatenate([past, new], axis=0)[None]

        kc = compress_tokens(full(c_cmp_k, kc1), pe_k, wk1, wk2)
        vc = compress_tokens(full(c_cmp_v, vc1), pe_v, wv1, wv2)
        ksb = to_blocks(full(c_slc_k, ks1))
        vsb = to_blocks(full(c_slc_v, vs1))
        kwb = jnp.concatenate([bk1, kw1], axis=0)[None]
        vwb = jnp.concatenate([bv1, vw1], axis=0)[None]
        o_n = nsa_core(q1[None], qr1[None], q_pos, kc, vc, cmp_end, ksb, vsb, kwb, vwb, kw_pos, g1[None])
        kd = full(c_diff_k, dk1).reshape(1, L, DIFF_HEADS, 2, DIFF_DIM)
        vd = full(c_diff_v, dv1)
        o_d = diff_core(dq1[None], kd, vd, q_pos, k_pos, lam)
        return o_n[0], o_d[0]

    return lax.map(one, (page_table, q, qr, kc_t, vc_t, ks, vs, kw, vw, gates, dq, dk_rows, dv,
                         c_win_k, c_win_v))


def mix_out(o_n, o_d, g_nsa, g_sub, w_out, lam_init):
    B, T = o_n.shape[:2]
    a = rms_norm(o_n, g_nsa.reshape(NSA_HEADS, HEAD_DIM))
    d = rms_norm(o_d, g_sub) * (1.0 - lam_init)
    return jnp.concatenate([a.reshape(B, T, NSA_Q), d.reshape(B, T, DIFF_W)], axis=-1) @ w_out


def mem_kv(mem, g_src, w_mk, w_mv):
    B, M, _ = mem.shape
    m = rms_norm(mem, g_src)
    return ((m @ w_mk).reshape(B, M, MEM_HEADS, MEM_HEAD_DIM),
            (m @ w_mv).reshape(B, M, MEM_HEADS, MEM_HEAD_DIM))


def mem_block(h, g, w_mq, mk, mv, w_mo):
    B, T, _ = h.shape
    q = (rms_norm(h, g) @ w_mq).reshape(B, T, MEM_HEADS, MEM_HEAD_DIM)
    s = jnp.einsum('bqhd,bmhd->bhqm', q, mk) * (MEM_HEAD_DIM ** -0.5)
    p = jax.nn.softmax(s.astype(jnp.float32), axis=-1)
    o = jnp.einsum('bhqm,bmhd->bqhd', p.astype(mv.dtype), mv)
    return h + o.reshape(B, T, MEM_W) @ w_mo


def peer_ffn(x, w_q, sub_keys, u, v):
    n = x.shape[0]
    pad = (-n) % PEER_CHUNK
    xp = jnp.pad(x, ((0, pad), (0, 0))).reshape(-1, PEER_CHUNK, D_MODEL)

    def chunk(xc):
        q = (xc @ w_q).reshape(PEER_CHUNK, PEER_HEADS, 2, PEER_DK // 2)
        s1 = jnp.einsum('thd,kd->thk', q[:, :, 0], sub_keys[0]).astype(jnp.float32)
        s2 = jnp.einsum('thd,kd->thk', q[:, :, 1], sub_keys[1]).astype(jnp.float32)
        v1, i1 = lax.top_k(s1, PEER_TOPK)
        v2, i2 = lax.top_k(s2, PEER_TOPK)
        cand = (v1[..., :, None] + v2[..., None, :]).reshape(PEER_CHUNK, PEER_HEADS, PEER_TOPK * PEER_TOPK)
        cidx = (i1[..., :, None] * PEER_KEYS + i2[..., None, :]).reshape(PEER_CHUNK, PEER_HEADS, PEER_TOPK * PEER_TOPK)
        sc, sel = lax.top_k(cand, PEER_TOPK)
        eidx = jnp.take_along_axis(cidx, sel, axis=-1)
        g = jax.nn.softmax(sc, axis=-1)
        ue = u[eidx]
        ve = v[eidx]
        act = jax.nn.gelu(jnp.einsum('td,thkd->thk', xc, ue).astype(jnp.float32), approximate=False)
        return jnp.einsum('thk,thkd->td', (g * act).astype(xc.dtype), ve)

    return lax.map(chunk, xp).reshape(-1, D_MODEL)[:n]


def ffn_block(h, g, w_q, sub_keys, u, v):
    B, T, D = h.shape
    return h + peer_ffn(rms_norm(h, g).reshape(B * T, D), w_q, sub_keys, u, v).reshape(B, T, D)


def _final_norm_body(x_ref, g_ref, o_ref):
    x = x_ref[...]
    y = x * lax.rsqrt(jnp.mean(x * x, axis=-1, keepdims=True) + RMS_EPS)
    o_ref[...] = y * g_ref[...]


def _final_norm(h, g):
    B, T, D = h.shape
    x = h.reshape(B * T, D)
    n = x.shape[0]
    tm = 512
    out = pl.pallas_call(
        _final_norm_body,
        grid=(n // tm,),
        in_specs=[pl.BlockSpec((tm, D), lambda i: (i, 0)), pl.BlockSpec((1, D), lambda i: (0, 0))],
        out_specs=pl.BlockSpec((tm, D), lambda i: (i, 0)),
        out_shape=jax.ShapeDtypeStruct((n, D), jnp.float32),
    )(x, g.reshape(1, D))
    return out.reshape(B, T, D)


def kernel(x_prompt, x_sample, cache_cmp_k, cache_cmp_v, cache_slc_k, cache_slc_v,
           cache_diff_k, cache_diff_v, cache_win_k, cache_win_v, cache_mem_k, cache_mem_v,
           page_table, mem_prompt, norm_mix, w_in, cmp_pe_k, cmp_pe_v, cmp_k_w1, cmp_k_w2,
           cmp_v_w1, cmp_v_w2, nsa_out_norm, diff_lq1, diff_lk1, diff_lq2, diff_lk2, diff_subln,
           w_out, norm_mem_q, norm_mem_src, w_mq, w_mk, w_mv, w_mo, norm_ffn, peer_wq, peer_keys,
           peer_u, peer_v, norm_final):
    f32 = jnp.float32
    hp, hs = x_prompt, x_sample
    pos_p = jnp.arange(x_prompt.shape[1])
    pos_s = PAST_LEN + jnp.arange(x_sample.shape[1])
    l = 0
    lam_init = 0.8 - 0.6 * math.exp(-0.3 * l)
    lam = (jnp.exp(jnp.sum(diff_lq1[l].astype(f32) * diff_lk1[l].astype(f32)))
           - jnp.exp(jnp.sum(diff_lq2[l].astype(f32) * diff_lk2[l].astype(f32))) + lam_init)
    cmp_w = (cmp_pe_k[l], cmp_k_w1[l], cmp_k_w2[l], cmp_pe_v[l], cmp_v_w1[l], cmp_v_w2[l])

    pp = project_in(rms_norm(hp, norm_mix[l]), w_in[l], pos_p)
    o_n, o_d = prompt_mixer(pp, lam, cmp_w)
    hp = hp + mix_out(o_n, o_d, nsa_out_norm[l], diff_subln[l], w_out[l], lam_init)
    mk, mv = mem_kv(mem_prompt, norm_mem_src[l], w_mk[l], w_mv[l])
    hp = mem_block(hp, norm_mem_q[l], w_mq[l], mk, mv, w_mo[l])
    hp = ffn_block(hp, norm_ffn[l], peer_wq[l], peer_keys[l], peer_u[l], peer_v[l])
    Bp, Tp = x_prompt.shape[:2]
    w_p = min(WINDOW, Tp)

    ps = project_in(rms_norm(hs, norm_mix[l]), w_in[l], pos_s)
    o_n, o_d = sample_mixer(ps, page_table, cache_cmp_k[l], cache_cmp_v[l], cache_slc_k[l],
                            cache_slc_v[l], cache_diff_k[l], cache_diff_v[l], cache_win_k[l],
                            cache_win_v[l], lam, cmp_w)
    hs = hs + mix_out(o_n, o_d, nsa_out_norm[l], diff_subln[l], w_out[l], lam_init)
    hs = mem_block(hs, norm_mem_q[l], w_mq[l], cache_mem_k[l], cache_mem_v[l], w_mo[l])
    hs = ffn_block(hs, norm_ffn[l], peer_wq[l], peer_keys[l], peer_u[l], peer_v[l])
    Bs, Ts = x_sample.shape[:2]

    y_prompt = _final_norm(hp, norm_final)
    y_sample = _final_norm(hs, norm_final)
    st = lambda t: t[None]
    return (y_prompt, y_sample,
            st(pp[2]), st(pp[3]), st(pp[4]), st(pp[5]), st(pp[6][:, Tp - w_p:]), st(pp[7][:, Tp - w_p:]),
            st(pp[10].reshape(Bp, Tp, DIFF_HEADS, 2 * DIFF_DIM)), st(pp[11]), st(mk), st(mv),
            st(ps[2]), st(ps[3]), st(ps[4]), st(ps[5]),
            st(jnp.concatenate([cache_win_k[l], ps[6]], axis=1)[:, Ts:]),
            st(jnp.concatenate([cache_win_v[l], ps[7]], axis=1)[:, Ts:]),
            st(ps[10].reshape(Bs, Ts, DIFF_HEADS, 2 * DIFF_DIM)), st(ps[11]))
```

```python
import functools
import math
import jax, jax.numpy as jnp
from jax import lax
import numpy as np
from jax.experimental import pallas as pl
from jax.experimental.pallas import tpu as pltpu

D_MODEL = 1024
BATCH = 2
SEQ = 8192
DEPTH = 1
DEC_BATCH = 128
DEC_SEQ = 8
PAST_LEN = 8192
PAGE_SIZE = 128
HEAD_DIM = 64
NSA_HEADS = 8
NSA_KV_HEADS = 2
CMP_LEN = 32
CMP_STRIDE = 16
CMP_HIDDEN = 128
SEL_LEN = 64
SEL_TOPK = 16
WINDOW = 512
DIFF_HEADS = 4
DIFF_DIM = 64
ROPE_THETA = 500000.0
ROT_FRAC = 4
MEM_HEADS = 4
MEM_HEAD_DIM = 128
PEER_HEADS = 8
PEER_KEYS = 128
PEER_EXPERTS = PEER_KEYS * PEER_KEYS
PEER_DK = 256
PEER_TOPK = 16
PEER_CHUNK = 256
Q_BLOCK = 128
RMS_EPS = 1e-6
NEG = -1e30
BIG = 1e9

NSA_Q = NSA_HEADS * HEAD_DIM
NSA_KV = NSA_KV_HEADS * HEAD_DIM
NSA_GATE = NSA_HEADS * 3
DIFF_W = DIFF_HEADS * 2 * DIFF_DIM
IN_SPLITS = (NSA_Q, NSA_KV, NSA_KV, NSA_KV, NSA_KV, NSA_KV, NSA_KV, NSA_GATE, DIFF_W, DIFF_W, DIFF_W)
IN_WIDTH = sum(IN_SPLITS)
MIX_WIDTH = NSA_Q + DIFF_W
MEM_W = MEM_HEADS * MEM_HEAD_DIM
MEM_LEN = 256


def rms_norm(x, g):
    xf = x.astype(jnp.float32)
    y = xf * lax.rsqrt(jnp.mean(xf * xf, axis=-1, keepdims=True) + RMS_EPS)
    return (y * g.astype(jnp.float32)).astype(x.dtype)


def rope(x, pos):
    d = x.shape[-1]
    rot = d // ROT_FRAC
    half = rot // 2
    inv = jnp.power(jnp.float32(ROPE_THETA), -jnp.arange(half, dtype=jnp.float32) * 2.0 / rot)
    ang = pos.astype(jnp.float32)[:, None] * inv
    shp = (ang.shape[0],) + (1,) * (x.ndim - 3) + (half,)
    cos = jnp.cos(ang).reshape(shp)
    sin = jnp.sin(ang).reshape(shp)
    xf = x.astype(jnp.float32)
    x1, x2, rest = xf[..., :half], xf[..., half:rot], xf[..., rot:]
    out = jnp.concatenate([x1 * cos - x2 * sin, x2 * cos + x1 * sin, rest], axis=-1)
    return out.astype(x.dtype)


def masked_softmax(s, mask):
    s = jnp.where(mask, s.astype(jnp.float32), NEG)
    return jnp.where(mask, jax.nn.softmax(s, axis=-1), 0.0)


def compress_tokens(tok, pe, w1, w2):
    B, L, G, dh = tok.shape
    nc = (L - CMP_LEN) // CMP_STRIDE + 1
    idx = jnp.arange(nc)[:, None] * CMP_STRIDE + jnp.arange(CMP_LEN)[None, :]
    blk = tok[:, idx] + pe[:, None, :]
    blk = blk.transpose(0, 1, 3, 2, 4).reshape(B, nc, G, CMP_LEN * dh)
    return jax.nn.gelu(blk @ w1, approximate=False) @ w2


def to_blocks(t):
    B, L, G, dh = t.shape
    ns = -(-L // SEL_LEN)
    t = jnp.pad(t, ((0, 0), (0, ns * SEL_LEN - L), (0, 0), (0, 0)))
    return t.reshape(B, ns, SEL_LEN, G, dh).transpose(0, 3, 1, 2, 4)


def nsa_core(q, qr, q_pos, kc, vc, cmp_end, ksb, vsb, kw, vw, kw_pos, gates):
    B, Tq, H, dh = q.shape
    G = kc.shape[2]
    R = H // G
    scale = dh ** -0.5
    qg = q.reshape(B, Tq, G, R, dh)
    qrg = qr.reshape(B, Tq, G, R, dh)
    s = jnp.einsum('bqgrd,bngd->bqgrn', qg, kc) * scale
    m = (cmp_end[None, :] <= q_pos[:, None])[None, :, None, None, :]
    p_cmp = masked_softmax(s, m)
    o_cmp = jnp.einsum('bqgrn,bngd->bqgrd', p_cmp.astype(vc.dtype), vc)
    nc = kc.shape[1]
    ns = ksb.shape[2]
    sub = SEL_LEN // CMP_STRIDE
    n_sub = ns * sub
    imp = p_cmp.sum(axis=3)
    seg = sum(jnp.pad(imp, ((0, 0), (0, 0), (0, 0), (r, n_sub - nc - r)))
              for r in range(CMP_LEN // CMP_STRIDE))
    imp_blk = seg.reshape(B, Tq, G, ns, sub).sum(-1)
    blk = jnp.arange(ns)[None, :]
    cur = (q_pos // SEL_LEN)[:, None]
    forced = (blk == 0) | (blk == cur) | (blk == cur - 1)
    valid = blk <= cur
    score = jnp.where(forced[None, :, None, :], BIG, jnp.where(valid[None, :, None, :], imp_blk, -BIG))
    n_sel = min(SEL_TOPK, ns)
    _, idx = lax.top_k(score, n_sel)
    bi = jnp.arange(B)[:, None, None, None]
    gi = jnp.arange(G)[None, None, :, None]
    kg = ksb[bi, gi, idx].reshape(B, Tq, G, n_sel * SEL_LEN, dh)
    vg = vsb[bi, gi, idx].reshape(B, Tq, G, n_sel * SEL_LEN, dh)
    kpos = (idx[..., None] * SEL_LEN + jnp.arange(SEL_LEN)).reshape(B, Tq, G, n_sel * SEL_LEN)
    s = jnp.einsum('bqgrd,bqgmd->bqgrm', qrg, kg) * scale
    p = masked_softmax(s, (kpos <= q_pos[None, :, None, None])[:, :, :, None, :])
    o_slc = jnp.einsum('bqgrm,bqgmd->bqgrd', p.astype(vg.dtype), vg)
    s = jnp.einsum('bqgrd,bkgd->bqgrk', qrg, kw) * scale
    dist = q_pos[:, None] - kw_pos[None, :]
    m = ((dist >= 0) & (dist < WINDOW) & (kw_pos >= 0)[None, :])[None, :, None, None, :]
    p = masked_softmax(s, m)
    o_win = jnp.einsum('bqgrk,bkgd->bqgrd', p.astype(vw.dtype), vw)
    g = jax.nn.sigmoid(gates.astype(jnp.float32)).reshape(B, Tq, G, R, 3).astype(q.dtype)
    o = g[..., 0:1] * o_cmp + g[..., 1:2] * o_slc + g[..., 2:3] * o_win
    return o.reshape(B, Tq, H, dh)


def diff_core(qd, kd, vd, q_pos, k_pos, lam):
    s = jnp.einsum('bqhmd,bkhmd->bhmqk', qd, kd) * (DIFF_DIM ** -0.5)
    p = masked_softmax(s, k_pos[None, :] <= q_pos[:, None])
    a = p[:, :, 0] - lam * p[:, :, 1]
    return jnp.einsum('bhqk,bkhe->bqhe', a.astype(vd.dtype), vd)


def project_in(hn, w_in, pos):
    B, T, _ = hn.shape
    offs = [int(o) for o in np.cumsum(IN_SPLITS)[:-1]]
    q, kc, vc, ks, vs, kw, vw, gl, dq, dk, dv = jnp.split(hn @ w_in, offs, axis=-1)
    q = q.reshape(B, T, NSA_HEADS, HEAD_DIM)
    kv = lambda t: t.reshape(B, T, NSA_KV_HEADS, HEAD_DIM)
    dq = dq.reshape(B, T, DIFF_HEADS, 2, DIFF_DIM)
    dk = dk.reshape(B, T, DIFF_HEADS, 2, DIFF_DIM)
    return (q, rope(q, pos), kv(kc), kv(vc), rope(kv(ks), pos), kv(vs), rope(kv(kw), pos), kv(vw),
            gl.reshape(B, T, NSA_HEADS, 3), rope(dq, pos), rope(dk, pos),
            dv.reshape(B, T, DIFF_HEADS, 2 * DIFF_DIM))


BF16 = jnp.bfloat16
F32 = jnp.float32
GQA_REP = NSA_HEADS // NSA_KV_HEADS
N_SEL_BLOCKS_PER_TILE = None


def _online_update(s, mask, v, m_ref, l_ref, acc_ref):
    m_old = m_ref[...]
    if mask is not None:
        s = jnp.where(mask, s, NEG)
    m_new = jnp.maximum(m_old, jnp.max(s, axis=-1, keepdims=True))
    p = jnp.exp(s - m_new)
    if mask is not None:
        p = jnp.where(mask, p, 0.0)
    alpha = jnp.exp(m_old - m_new)
    l_ref[...] = alpha * l_ref[...] + jnp.sum(p, axis=-1, keepdims=True)
    lead = p.shape[:-2]
    p2 = p.astype(BF16).reshape((-1, p.shape[-1]))
    pv = jnp.dot(p2, v, preferred_element_type=F32).reshape(lead + (p.shape[-2], v.shape[-1]))
    acc_ref[...] = alpha * acc_ref[...] + pv
    m_ref[...] = m_new


def _diff_body(lam_ref, q_ref, k_ref, v_ref, o_ref, m_sc, l_sc, acc_sc, *, tq):
    i = pl.program_id(2)
    j = pl.program_id(3)

    @pl.when(j == 0)
    def _():
        m_sc[...] = jnp.full(m_sc.shape, NEG, F32)
        l_sc[...] = jnp.zeros(l_sc.shape, F32)
        acc_sc[...] = jnp.zeros(acc_sc.shape, F32)

    def step(masked):
        v = v_ref[0]
        if masked:
            row = lax.broadcasted_iota(jnp.int32, (tq, tq), 0)
            col = lax.broadcasted_iota(jnp.int32, (tq, tq), 1)
            mask = col <= row
        else:
            mask = None
        for mp in range(2):
            s = lax.dot_general(q_ref[0, 0, mp], k_ref[0, 0, mp], (((1,), (1,)), ((), ())),
                                preferred_element_type=F32)
            _online_update(s, mask, v, m_sc.at[mp], l_sc.at[mp], acc_sc.at[mp])

    @pl.when(j < i)
    def _():
        step(False)

    @pl.when(j == i)
    def _():
        step(True)
        lam = lam_ref[0]
        o_ref[0] = acc_sc[0] / l_sc[0] - lam * (acc_sc[1] / l_sc[1])


def diff_attention_causal(dq, dk, dv, lam, *, tq=512):
    B, H, _, T, dd = dq.shape
    nq = T // tq
    body = lambda *a: _diff_body(*a, tq=tq)
    return pl.pallas_call(
        body,
        grid=(B, H, nq, nq),
        in_specs=[
            pl.BlockSpec(memory_space=pltpu.SMEM),
            pl.BlockSpec((1, 1, 2, tq, dd), lambda b, h, i, j: (b, h, 0, i, 0)),
            pl.BlockSpec((1, 1, 2, tq, dd), lambda b, h, i, j: (b, h, 0, jnp.minimum(j, i), 0)),
            pl.BlockSpec((1, tq, 2 * dd), lambda b, h, i, j: (b, jnp.minimum(j, i), h)),
        ],
        out_specs=pl.BlockSpec((1, tq, 2 * dd), lambda b, h, i, j: (b, i, h)),
        out_shape=jax.ShapeDtypeStruct((B, T, H * 2 * dd), F32),
        scratch_shapes=[pltpu.VMEM((2, tq, 1), F32), pltpu.VMEM((2, tq, 1), F32),
                        pltpu.VMEM((2, tq, 2 * dd), F32)],
        compiler_params=pltpu.CompilerParams(
            dimension_semantics=("parallel", "parallel", "parallel", "arbitrary")),
        name="diff_attention",
    )(lam.reshape(1).astype(F32), dq, dk, dv)


def _cmp_select_body(q_ref, kc_ref, vc_ref, segmat_ref, o_ref, sel_ref, *, tq, n_sel):
    i = pl.program_id(2)
    R = q_ref.shape[2]
    ncp = kc_ref.shape[2]
    ns = sel_ref.shape[3]
    q0 = i * tq
    q_pos = q0 + lax.broadcasted_iota(jnp.int32, (tq, 1), 0)
    cmp_end = lax.broadcasted_iota(jnp.int32, (1, ncp), 1) * CMP_STRIDE + (CMP_LEN - 1)
    mask = cmp_end <= q_pos
    kc = kc_ref[0, 0]
    vc = vc_ref[0, 0]
    imp = jnp.zeros((tq, ncp), F32)
    for r in range(R):
        s = lax.dot_general(q_ref[0, 0, r], kc, (((1,), (1,)), ((), ())), preferred_element_type=F32)
        s = jnp.where(mask, s, NEG)
        m = jnp.max(s, axis=-1, keepdims=True)
        p = jnp.where(mask, jnp.exp(s - m), 0.0)
        l = jnp.sum(p, axis=-1, keepdims=True)
        p = p * jnp.where(l > 0.0, 1.0 / l, 0.0)
        o_ref[0, 0, r] = jnp.dot(p.astype(BF16), vc, preferred_element_type=F32)
        imp = imp + p
    imp_blk = jnp.dot(imp, segmat_ref[...], preferred_element_type=F32,
                      precision=lax.Precision.HIGHEST)
    sel_ref[0, 0] = _select_blocks(imp_blk, q_pos, ns, n_sel).astype(sel_ref.dtype)


def _select_blocks(imp_blk, q_pos, n_blocks, n_sel):
    width = imp_blk.shape[-1]
    blk = lax.broadcasted_iota(jnp.int32, (1, width), 1)
    cur = q_pos // SEL_LEN
    forced = (blk == 0) | (blk == cur) | (blk == cur - 1)
    valid = blk <= cur
    work = jnp.where(forced, BIG, jnp.where(valid, imp_blk, -BIG))
    if width > n_blocks:
        work = jnp.where(blk < n_blocks, work, -3.0e38)
    sel = jnp.zeros(imp_blk.shape, F32)
    for _ in range(n_sel):
        mx = jnp.max(work, axis=-1, keepdims=True)
        first = jnp.min(jnp.where(work == mx, blk, width), axis=-1, keepdims=True)
        pick = blk == first
        sel = jnp.where(pick, 1.0, sel)
        work = jnp.where(pick, -3.0e38, work)
    return sel


def _segment_matrix(ncp, ns):
    sub = SEL_LEN // CMP_STRIDE
    n = np.arange(ncp)[:, None]
    j = np.arange(ns)[None, :]
    m = ((n >= sub * j) & (n < sub * j + sub)).astype(np.float32)
    for r in range(1, CMP_LEN // CMP_STRIDE):
        m = m + ((n + r >= sub * j) & (n + r < sub * j + sub)).astype(np.float32)
    return jnp.asarray(m, F32)


def nsa_cmp_select(q, kc, vc, ns, *, tq=256):
    B, G, R, T, dh = q.shape
    ncp = kc.shape[2]
    n_sel = min(SEL_TOPK, ns)
    body = lambda *a: _cmp_select_body(*a, tq=tq, n_sel=n_sel)
    return pl.pallas_call(
        body,
        grid=(B, G, T // tq),
        in_specs=[
            pl.BlockSpec((1, 1, R, tq, dh), lambda b, g, i: (b, g, 0, i, 0)),
            pl.BlockSpec((1, 1, ncp, dh), lambda b, g, i: (b, g, 0, 0)),
            pl.BlockSpec((1, 1, ncp, dh), lambda b, g, i: (b, g, 0, 0)),
            pl.BlockSpec((ncp, ns), lambda b, g, i: (0, 0)),
        ],
        out_specs=[
            pl.BlockSpec((1, 1, R, tq, dh), lambda b, g, i: (b, g, 0, i, 0)),
            pl.BlockSpec((1, 1, tq, ns), lambda b, g, i: (b, g, i, 0)),
        ],
        out_shape=[jax.ShapeDtypeStruct((B, G, R, T, dh), F32),
                   jax.ShapeDtypeStruct((B, G, T, ns), BF16)],
        compiler_params=pltpu.CompilerParams(dimension_semantics=("parallel", "parallel", "parallel")),
        name="nsa_cmp_select",
    )(q, kc, vc, _segment_matrix(ncp, ns))


def _slc_win_body(q_ref, ks_ref, vs_ref, kw_ref, vw_ref, sel_ref, exp_ref, os_ref, ow_ref,
                  ms_sc, ls_sc, as_sc, mw_sc, lw_sc, aw_sc, *, tq, tk):
    i = pl.program_id(2)
    j = pl.program_id(3)
    R = q_ref.shape[2]
    dh = q_ref.shape[4]
    q0 = i * tq
    last = (q0 + tq - 1) // tk
    first_win = jnp.maximum((q0 - (WINDOW - 1)) // tk, 0)

    @pl.when(j == 0)
    def _():
        for m_sc, l_sc, a_sc in ((ms_sc, ls_sc, as_sc), (mw_sc, lw_sc, aw_sc)):
            m_sc[...] = jnp.full(m_sc.shape, NEG, F32)
            l_sc[...] = jnp.zeros(l_sc.shape, F32)
            a_sc[...] = jnp.zeros(a_sc.shape, F32)

    def positions():
        q_pos = q0 + lax.broadcasted_iota(jnp.int32, (tq, tk), 0)
        k_pos = j * tk + lax.broadcasted_iota(jnp.int32, (tq, tk), 1)
        return q_pos, k_pos

    q = q_ref[0, 0].reshape(R * tq, dh)

    def scores(k_ref):
        s = lax.dot_general(q, k_ref[0, 0], (((1,), (1,)), ((), ())), preferred_element_type=F32)
        return s.reshape(R, tq, tk)

    def slc_step(diagonal):
        picked = jnp.dot(sel_ref[0, 0], exp_ref[...], preferred_element_type=F32) > 0.5
        if diagonal:
            q_pos, k_pos = positions()
            picked = picked & (k_pos <= q_pos)
        _online_update(scores(ks_ref), picked[None], vs_ref[0, 0], ms_sc, ls_sc, as_sc)

    def win_step():
        q_pos, k_pos = positions()
        dist = q_pos - k_pos
        mask = (dist >= 0) & (dist < WINDOW)
        _online_update(scores(kw_ref), mask[None], vw_ref[0, 0], mw_sc, lw_sc, aw_sc)

    @pl.when(j < last)
    def _():
        slc_step(False)

    @pl.when((j >= first_win) & (j <= last))
    def _():
        win_step()

    @pl.when(j == last)
    def _():
        slc_step(True)
        os_ref[0, 0] = as_sc[...] / ls_sc[...]
        ow_ref[0, 0] = aw_sc[...] / lw_sc[...]


def nsa_slc_win(qr, ks, vs, kw, vw, sel, *, tq=256, tk=512):
    B, G, R, T, dh = qr.shape
    ns = sel.shape[3]
    nk = T // tk
    expand = (jnp.arange(T)[None, :] // SEL_LEN == jnp.arange(ns)[:, None]).astype(BF16)
    last = lambda i: (i * tq + tq - 1) // tk
    kidx = lambda b, g, i, j: (b, g, jnp.minimum(j, last(i)), 0)
    widx = lambda b, g, i, j: (b, g, jnp.clip(j, jnp.maximum((i * tq - (WINDOW - 1)) // tk, 0), last(i)), 0)
    body = lambda *a: _slc_win_body(*a, tq=tq, tk=tk)
    o_spec = pl.BlockSpec((1, 1, R, tq, dh), lambda b, g, i, j: (b, g, 0, i, 0))
    return pl.pallas_call(
        body,
        grid=(B, G, T // tq, nk),
        in_specs=[
            pl.BlockSpec((1, 1, R, tq, dh), lambda b, g, i, j: (b, g, 0, i, 0)),
            pl.BlockSpec((1, 1, tk, dh), kidx),
            pl.BlockSpec((1, 1, tk, dh), kidx),
            pl.BlockSpec((1, 1, tk, dh), widx),
            pl.BlockSpec((1, 1, tk, dh), widx),
            pl.BlockSpec((1, 1, tq, ns), lambda b, g, i, j: (b, g, i, 0)),
            pl.BlockSpec((ns, tk), lambda b, g, i, j: (0, jnp.minimum(j, last(i)))),
        ],
        out_specs=[o_spec, o_spec],
        out_shape=[jax.ShapeDtypeStruct((B, G, R, T, dh), F32)] * 2,
        scratch_shapes=[pltpu.VMEM((R, tq, 1), F32), pltpu.VMEM((R, tq, 1), F32), pltpu.VMEM((R, tq, dh), F32),
                        pltpu.VMEM((R, tq, 1), F32), pltpu.VMEM((R, tq, 1), F32), pltpu.VMEM((R, tq, dh), F32)],
        compiler_params=pltpu.CompilerParams(
            dimension_semantics=("parallel", "parallel", "parallel", "arbitrary")),
        name="nsa_slc_win",
    )(qr, ks, vs, kw, vw, sel, expand)


def _heads_major(t, scale=None):
    B, T, H, dh = t.shape
    if scale is not None:
        t = t * scale
    return t.astype(BF16).reshape(B, T, NSA_KV_HEADS, H // NSA_KV_HEADS, dh).transpose(0, 2, 3, 1, 4)


def _groups_major(t):
    return t.astype(BF16).transpose(0, 2, 1, 3)


def prompt_mixer(pp, lam, cmp_w):
    q, qr, kc_t, vc_t, ks, vs, kw, vw, gates, dq, dk, dv = pp
    B, T = q.shape[:2]
    pe_k, wk1, wk2, pe_v, wv1, wv2 = cmp_w
    kc = compress_tokens(kc_t, pe_k, wk1, wk2)
    vc = compress_tokens(vc_t, pe_v, wv1, wv2)
    nc = kc.shape[1]
    ns = -(-T // SEL_LEN)
    ncp = ns * (SEL_LEN // CMP_STRIDE)
    pad_c = lambda t: _groups_major(jnp.pad(t, ((0, 0), (0, ncp - nc), (0, 0), (0, 0))))
    scale = HEAD_DIM ** -0.5
    o_cmp, sel = nsa_cmp_select(_heads_major(q, scale), pad_c(kc), pad_c(vc), ns)
    o_slc, o_win = nsa_slc_win(_heads_major(qr, scale), _groups_major(ks), _groups_major(vs),
                               _groups_major(kw), _groups_major(vw), sel)
    back = lambda t: t.transpose(0, 3, 1, 2, 4)
    g = jax.nn.sigmoid(gates.astype(F32)).reshape(B, T, NSA_KV_HEADS, GQA_REP, 3)
    o_n = (g[..., 0:1] * back(o_cmp) + g[..., 1:2] * back(o_slc) + g[..., 2:3] * back(o_win))
    o_n = o_n.reshape(B, T, NSA_HEADS, HEAD_DIM)

    dscale = DIFF_DIM ** -0.5
    dqh = (dq * dscale).astype(BF16).transpose(0, 2, 3, 1, 4)
    dkh = dk.astype(BF16).transpose(0, 2, 3, 1, 4)
    o_d = diff_attention_causal(dqh, dkh, dv.reshape(B, T, DIFF_W).astype(BF16), lam)
    return o_n, o_d.reshape(B, T, DIFF_HEADS, 2 * DIFF_DIM)


def sample_mixer(ps, page_table, c_cmp_k, c_cmp_v, c_slc_k, c_slc_v, c_diff_k, c_diff_v,
                 c_win_k, c_win_v, lam, cmp_w):
    q, qr, kc_t, vc_t, ks, vs, kw, vw, gates, dq, dk, dv = ps
    Bd, T = q.shape[:2]
    G, R = NSA_KV_HEADS, GQA_REP
    L = PAST_LEN + T
    n_pages = PAST_LEN // PAGE_SIZE
    chunks_per_page = PAGE_SIZE // CMP_STRIDE
    n_chunks = PAST_LEN // CMP_STRIDE
    nc = (L - CMP_LEN) // CMP_STRIDE + 1
    ns = -(-L // SEL_LEN)
    assert nc == n_chunks - 1 and T <= PAGE_SIZE and PAST_LEN % SEL_LEN == 0 and CMP_LEN == 2 * CMP_STRIDE
    pe_k, wk1, wk2, pe_v, wv1, wv2 = cmp_w

    pq_k = _chunk_hidden(c_cmp_k, wk1)
    pq_v = _chunk_hidden(c_cmp_v, wv1)
    bias_k = (pe_k.reshape(1, -1) @ wk1).astype(F32)
    bias_v = (pe_v.reshape(1, -1) @ wv1).astype(F32)

    rows = lambda t, n: t.reshape(Bd, T, n, -1).transpose(0, 2, 1, 3)
    q_rows = (rows(q, NSA_HEADS) * HEAD_DIM ** -0.5).astype(BF16).reshape(Bd, G * R * T, HEAD_DIM)
    o_cmp, sel = _sample_cmp_select(page_table, pq_k, pq_v, bias_k, bias_v, wk2.astype(BF16),
                                    wv2.astype(BF16), q_rows, ns, nc)

    def block_diag(t, n):
        d = t.shape[-1]
        eye = jnp.eye(n, dtype=t.dtype)[None, :, None, :, None]
        return (t[:, :, :, None, :] * eye).reshape(Bd, n * t.shape[2], n * d)

    qs_rows = (rows(qr, NSA_HEADS) * HEAD_DIM ** -0.5).reshape(Bd, G, R * T, HEAD_DIM)
    qs_blk = block_diag(qs_rows, G).astype(BF16)
    qd_rows = (dq * DIFF_DIM ** -0.5).transpose(0, 2, 3, 1, 4).reshape(Bd, DIFF_HEADS * 2, T, DIFF_DIM)
    qd_blk = block_diag(qd_rows, DIFF_HEADS * 2).astype(BF16)
    sel_rows = jnp.broadcast_to(sel.reshape(Bd, G, 1, T, -1), (Bd, G, R, T, sel.shape[-1]))
    sel_rows = sel_rows.reshape(Bd, G * R * T, -1)
    new_page = lambda t: jnp.pad(t.reshape(Bd, T, -1), ((0, 0), (0, PAGE_SIZE - T), (0, 0))).astype(BF16)
    flat = lambda c: c.reshape(c.shape[0], c.shape[1], -1)
    od, os_, ow = _sample_attend(page_table, flat(c_diff_k), flat(c_diff_v), flat(c_slc_k), flat(c_slc_v),
                                 qd_blk, qs_blk, sel_rows, new_page(dk), new_page(dv), new_page(ks),
                                 new_page(vs), new_page(kw), new_page(vw), flat(c_win_k), flat(c_win_v))

    own = lambda t, n: jnp.einsum('bnxnd->bnxd', t.reshape(Bd, n, -1, n, t.shape[-1] // n))
    o_slc = own(os_, G).reshape(Bd, G, R, T, HEAD_DIM)
    o_win = own(ow, G).reshape(Bd, G, R, T, HEAD_DIM)
    o_cmp = o_cmp.reshape(Bd, G, R, T, HEAD_DIM)
    back = lambda t: t.transpose(0, 3, 1, 2, 4)
    g = jax.nn.sigmoid(gates.astype(F32)).reshape(Bd, T, G, R, 3)
    o_n = g[..., 0:1] * back(o_cmp) + g[..., 1:2] * back(o_slc) + g[..., 2:3] * back(o_win)
    od = own(od, DIFF_HEADS).reshape(Bd, DIFF_HEADS, 2, T, 2 * DIFF_DIM)
    o_d = (od[:, :, 0] - lam * od[:, :, 1]).transpose(0, 2, 1, 3)
    return o_n.reshape(Bd, T, NSA_HEADS, HEAD_DIM), o_d


def _rows_matmul_body(x_ref, w_ref, o_ref):
    o_ref[...] = jnp.dot(x_ref[...].astype(BF16), w_ref[...], preferred_element_type=F32)


def _chunk_hidden(cache, w1, *, tm=1024):
    P, _, G, dh = cache.shape
    H = w1.shape[1]
    half = CMP_STRIDE * dh
    x = cache.reshape(P * (PAGE_SIZE // CMP_STRIDE), CMP_STRIDE * G * dh)
    w_ab = jnp.concatenate([w1[:half].reshape(CMP_STRIDE, dh, H), w1[half:].reshape(CMP_STRIDE, dh, H)], axis=-1)
    eye = jnp.eye(G, dtype=w1.dtype)
    w_big = (w_ab[:, None, :, None, :] * eye[None, :, None, :, None]).reshape(CMP_STRIDE * G * dh, G * 2 * H)
    M, K = x.shape
    N = w_big.shape[1]
    tm = min(tm, M)
    assert M % tm == 0
    return pl.pallas_call(
        _rows_matmul_body,
        grid=(M // tm,),
        in_specs=[pl.BlockSpec((tm, K), lambda i: (i, 0)), pl.BlockSpec((K, N), lambda i: (0, 0))],
        out_specs=pl.BlockSpec((tm, N), lambda i: (i, 0)),
        out_shape=jax.ShapeDtypeStruct((M, N), F32),
        compiler_params=pltpu.CompilerParams(dimension_semantics=("parallel",),
                                             vmem_limit_bytes=(V7X_VMEM_BYTES * 3) // 4),
        name="chunk_hidden",
    )(x, w_big.astype(BF16))


def _page_specs(block, pages_per_step):
    def index(b, j, pt_ref, *, s):
        return (pt_ref[b, j * pages_per_step + s],) + (0,) * len(block)
    return [pl.BlockSpec((1,) + block, functools.partial(index, s=s)) for s in range(pages_per_step)]


def _seq_spec(block):
    return pl.BlockSpec((1,) + block, lambda b, j, pt_ref: (b,) + (0,) * len(block))


def _const_spec(block):
    return pl.BlockSpec(block, lambda b, j, pt_ref: (0,) * len(block))


NT_DIMS = (((1,), (1,)), ((), ()))


def _sample_cmp_body(pt_ref, *refs, pps, n_steps, n_real, n_blocks, n_sel):
    pqk, pqv = refs[:pps], refs[pps:2 * pps]
    bias_k, bias_v, w2k, w2v, q_ref, seg_ref, o_ref, sel_ref, pqk_sc, pqv_sc = refs[2 * pps:]
    j = pl.program_id(1)
    cpp = pqk[0].shape[1]
    for s in range(pps):
        row0 = pl.multiple_of((j * pps + s) * cpp, cpp)
        pqk_sc[pl.ds(row0, cpp), :] = pqk[s][0]
        pqv_sc[pl.ds(row0, cpp), :] = pqv[s][0]

    @pl.when(j == n_steps - 1)
    def _():
        n_rows = pqk_sc.shape[0]
        H = w2k.shape[0]
        G = pqk_sc.shape[1] // (2 * H)
        RT = q_ref.shape[1] // G
        T = sel_ref.shape[1] // G
        visible = lax.broadcasted_iota(jnp.int32, (1, n_rows), 1) < n_real
        q_pos = PAST_LEN + lax.broadcasted_iota(jnp.int32, (T, 1), 0)

        def compress(sc, bias_ref, w2_ref, g):
            first = sc[:, g * 2 * H:g * 2 * H + H]
            second = pltpu.roll(sc[:, g * 2 * H + H:(g + 1) * 2 * H], n_rows - 1, 0)
            hid = _gelu_exact(first + second + bias_ref[...])
            return jnp.dot(hid.astype(BF16), w2_ref[...], preferred_element_type=F32).astype(BF16)

        for g in range(G):
            kc = compress(pqk_sc, bias_k, w2k, g)
            vc = compress(pqv_sc, bias_v, w2v, g)
            s = lax.dot_general(q_ref[0, g * RT:(g + 1) * RT, :], kc, NT_DIMS, preferred_element_type=F32)
            s = jnp.where(visible, s, NEG)
            m = jnp.max(s, axis=-1, keepdims=True)
            p = jnp.where(visible, jnp.exp(s - m), 0.0)
            p = p / jnp.sum(p, axis=-1, keepdims=True)
            o_ref[0, g * RT:(g + 1) * RT, :] = jnp.dot(p.astype(BF16), vc, preferred_element_type=F32)
            imp = jnp.sum(p.reshape(RT // T, T, n_rows), axis=0)
            imp_blk = jnp.dot(imp, seg_ref[...], preferred_element_type=F32, precision=lax.Precision.HIGHEST)
            sel_ref[0, g * T:(g + 1) * T, :] = _select_blocks(imp_blk, q_pos, n_blocks, n_sel).astype(sel_ref.dtype)


def _sample_cmp_select(page_table, pq_k, pq_v, bias_k, bias_v, w2k, w2v, q_rows, ns, nc, *, pps=16):
    Bd, n_pages = page_table.shape
    cpp = PAGE_SIZE // CMP_STRIDE
    n_rows = n_pages * cpp
    GRT, dh = q_rows.shape[1:]
    G = NSA_KV_HEADS
    T = GRT // (G * GQA_REP)
    H = w2k.shape[0]
    width = pq_k.shape[1]
    ns_pad = -(-ns // 128) * 128
    pps = min(pps, n_pages)
    n_steps = n_pages // pps
    pq_k = pq_k.reshape(-1, cpp, width)
    pq_v = pq_v.reshape(-1, cpp, width)
    body = functools.partial(_sample_cmp_body, pps=pps, n_steps=n_steps, n_real=nc, n_blocks=ns,
                             n_sel=min(SEL_TOPK, ns))
    grid_spec = pltpu.PrefetchScalarGridSpec(
        num_scalar_prefetch=1,
        grid=(Bd, n_steps),
        in_specs=(_page_specs((cpp, width), pps) + _page_specs((cpp, width), pps)
                  + [_const_spec((1, H)), _const_spec((1, H)), _const_spec((H, dh)), _const_spec((H, dh)),
                     _seq_spec((GRT, dh)), _const_spec((n_rows, ns_pad))]),
        out_specs=[_seq_spec((GRT, dh)), _seq_spec((G * T, ns_pad))],
        scratch_shapes=[pltpu.VMEM((n_rows, width), F32), pltpu.VMEM((n_rows, width), F32)],
    )
    return pl.pallas_call(
        body,
        grid_spec=grid_spec,
        out_shape=[jax.ShapeDtypeStruct((Bd, GRT, dh), F32), jax.ShapeDtypeStruct((Bd, G * T, ns_pad), BF16)],
        compiler_params=pltpu.CompilerParams(dimension_semantics=("parallel", "arbitrary")),
        name="sample_cmp_select",
    )(page_table, *([pq_k] * pps), *([pq_v] * pps), bias_k, bias_v, w2k, w2v, q_rows,
      _segment_matrix(n_rows, ns_pad))


def _sample_attend_body(pt_ref, *refs, pps, n_steps, n_new, w_buf):
    dk, dv, ksr, vsr = (refs[i * pps:(i + 1) * pps] for i in range(4))
    (qd_ref, qs_ref, sel_ref, exp_ref, dkn_ref, dvn_ref, ksn_ref, vsn_ref, kwn_ref, vwn_ref, wk_ref, wv_ref,
     od_ref, os_ref, ow_ref, md, ld, ad, ms, ls, as_) = refs[4 * pps:]
    j = pl.program_id(1)
    page = PAGE_SIZE

    @pl.when(j == 0)
    def _():
        for m_sc, l_sc, a_sc in ((md, ld, ad), (ms, ls, as_)):
            m_sc[...] = jnp.full(m_sc.shape, NEG, F32)
            l_sc[...] = jnp.zeros(l_sc.shape, F32)
            a_sc[...] = jnp.zeros(a_sc.shape, F32)

    qd = qd_ref[0]
    qs = qs_ref[0]
    sel = sel_ref[0]
    scores = lambda q, k: lax.dot_general(q, k.astype(BF16), NT_DIMS, preferred_element_type=F32)
    cat = lambda refs_, axis: jnp.concatenate([r[0].astype(BF16) for r in refs_], axis=axis)

    sd = jnp.concatenate([scores(qd, r[0]) for r in dk], axis=1)
    _online_update(sd, None, cat(dv, 0), md, ld, ad)
    ss = jnp.concatenate([scores(qs, r[0]) for r in ksr], axis=1)
    col0 = pl.multiple_of(j * (pps * page), pps * page)
    picked = jnp.dot(sel, exp_ref[:, pl.ds(col0, pps * page)], preferred_element_type=F32) > 0.5
    _online_update(ss, picked, cat(vsr, 0), ms, ls, as_)

    @pl.when(j == n_steps - 1)
    def _():
        n_rows = qd.shape[0]
        t_idx = lax.broadcasted_iota(jnp.int32, (n_rows, page), 0) % n_new
        k_idx = lax.broadcasted_iota(jnp.int32, (n_rows, page), 1)
        causal = k_idx <= t_idx
        _online_update(scores(qd, dkn_ref[0]), causal, dvn_ref[0], md, ld, ad)
        od_ref[0] = ad[...] / ld[...]
        picked_new = (jnp.dot(sel, exp_ref[:, pl.ds(n_steps * pps * page, page)],
                              preferred_element_type=F32) > 0.5) & causal
        _online_update(scores(qs, ksn_ref[0]), picked_new, vsn_ref[0], ms, ls, as_)
        os_ref[0] = as_[...] / ls[...]
        tb = lax.broadcasted_iota(jnp.int32, (n_rows, w_buf), 0) % n_new
        kb = lax.broadcasted_iota(jnp.int32, (n_rows, w_buf), 1)
        mask = jnp.concatenate([(w_buf + tb - kb) < WINDOW, causal], axis=1)
        s = jnp.concatenate([scores(qs, wk_ref[0]), scores(qs, kwn_ref[0])], axis=1)
        s = jnp.where(mask, s, NEG)
        m = jnp.max(s, axis=-1, keepdims=True)
        p = jnp.where(mask, jnp.exp(s - m), 0.0)
        v = jnp.concatenate([wv_ref[0].astype(BF16), vwn_ref[0]], axis=0)
        ow_ref[0] = jnp.dot(p.astype(BF16), v, preferred_element_type=F32) / jnp.sum(p, axis=-1, keepdims=True)


def _sample_attend(page_table, c_dk, c_dv, c_ks, c_vs, qd_blk, qs_blk, sel_rows, dk_new, dv_new, ks_new, vs_new,
                   kw_new, vw_new, win_k, win_v, *, pps=8):
    Bd, n_pages = page_table.shape
    pps = min(pps, n_pages)
    n_steps = n_pages // pps
    w_buf = win_k.shape[1]
    dw, sw = c_dk.shape[2], c_ks.shape[2]
    rd, rs = qd_blk.shape[1], qs_blk.shape[1]
    ns_pad = sel_rows.shape[2]
    n_keys = (n_pages + 1) * PAGE_SIZE
    n_new = rs // NSA_HEADS
    expand = (jnp.arange(n_keys)[None, :] // SEL_LEN == jnp.arange(ns_pad)[:, None]).astype(BF16)
    body = functools.partial(_sample_attend_body, pps=pps, n_steps=n_steps, n_new=n_new, w_buf=w_buf)
    page = PAGE_SIZE
    grid_spec = pltpu.PrefetchScalarGridSpec(
        num_scalar_prefetch=1,
        grid=(Bd, n_steps),
        in_specs=(_page_specs((page, dw), pps) + _page_specs((page, dw), pps)
                  + _page_specs((page, sw), pps) + _page_specs((page, sw), pps)
                  + [_seq_spec((rd, dw)), _seq_spec((rs, sw)), _seq_spec((rs, ns_pad)),
                     _const_spec((ns_pad, n_keys)),
                     _seq_spec((page, dw)), _seq_spec((page, dw)), _seq_spec((page, sw)), _seq_spec((page, sw)),
                     _seq_spec((page, sw)), _seq_spec((page, sw)), _seq_spec((w_buf, sw)), _seq_spec((w_buf, sw))]),
        out_specs=[_seq_spec((rd, dw)), _seq_spec((rs, sw)), _seq_spec((rs, sw))],
        scratch_shapes=[pltpu.VMEM((rd, 1), F32), pltpu.VMEM((rd, 1), F32), pltpu.VMEM((rd, dw), F32),
                        pltpu.VMEM((rs, 1), F32), pltpu.VMEM((rs, 1), F32), pltpu.VMEM((rs, sw), F32)],
    )
    return pl.pallas_call(
        body,
        grid_spec=grid_spec,
        out_shape=[jax.ShapeDtypeStruct((Bd, rd, dw), F32), jax.ShapeDtypeStruct((Bd, rs, sw), F32),
                   jax.ShapeDtypeStruct((Bd, rs, sw), F32)],
        compiler_params=pltpu.CompilerParams(dimension_semantics=("parallel", "arbitrary"),
                                             vmem_limit_bytes=(V7X_VMEM_BYTES * 3) // 4),
        name="sample_attend",
    )(page_table, *([c_dk] * pps), *([c_dv] * pps), *([c_ks] * pps), *([c_vs] * pps),
      qd_blk, qs_blk, sel_rows, expand, dk_new, dv_new, ks_new, vs_new, kw_new, vw_new, win_k, win_v)


def mix_out(o_n, o_d, g_nsa, g_sub, w_out, lam_init):
    B, T = o_n.shape[:2]
    a = rms_norm(o_n, g_nsa.reshape(NSA_HEADS, HEAD_DIM))
    d = rms_norm(o_d, g_sub) * (1.0 - lam_init)
    return jnp.concatenate([a.reshape(B, T, NSA_Q), d.reshape(B, T, DIFF_W)], axis=-1) @ w_out


def mem_kv(mem, g_src, w_mk, w_mv):
    B, M, _ = mem.shape
    m = rms_norm(mem, g_src)
    return ((m @ w_mk).reshape(B, M, MEM_HEADS, MEM_HEAD_DIM),
            (m @ w_mv).reshape(B, M, MEM_HEADS, MEM_HEAD_DIM))


def mem_block(h, g, w_mq, mk, mv, w_mo):
    B, T, _ = h.shape
    q = (rms_norm(h, g) @ w_mq).reshape(B, T, MEM_HEADS, MEM_HEAD_DIM)
    s = jnp.einsum('bqhd,bmhd->bhqm', q, mk) * (MEM_HEAD_DIM ** -0.5)
    p = jax.nn.softmax(s.astype(jnp.float32), axis=-1)
    o = jnp.einsum('bhqm,bmhd->bqhd', p.astype(mv.dtype), mv)
    return h + o.reshape(B, T, MEM_W) @ w_mo


def peer_ffn(x, w_q, sub_keys, u, v):
    n = x.shape[0]
    pad = (-n) % PEER_CHUNK
    xp = jnp.pad(x, ((0, pad), (0, 0))).reshape(-1, PEER_CHUNK, D_MODEL)

    def chunk(xc):
        q = (xc @ w_q).reshape(PEER_CHUNK, PEER_HEADS, 2, PEER_DK // 2)
        s1 = jnp.einsum('thd,kd->thk', q[:, :, 0], sub_keys[0]).astype(jnp.float32)
        s2 = jnp.einsum('thd,kd->thk', q[:, :, 1], sub_keys[1]).astype(jnp.float32)
        v1, i1 = lax.top_k(s1, PEER_TOPK)
        v2, i2 = lax.top_k(s2, PEER_TOPK)
        cand = (v1[..., :, None] + v2[..., None, :]).reshape(PEER_CHUNK, PEER_HEADS, PEER_TOPK * PEER_TOPK)
        cidx = (i1[..., :, None] * PEER_KEYS + i2[..., None, :]).reshape(PEER_CHUNK, PEER_HEADS, PEER_TOPK * PEER_TOPK)
        sc, sel = lax.top_k(cand, PEER_TOPK)
        eidx = jnp.take_along_axis(cidx, sel, axis=-1)
        g = jax.nn.softmax(sc, axis=-1)
        ue = u[eidx]
        ve = v[eidx]
        act = jax.nn.gelu(jnp.einsum('td,thkd->thk', xc, ue).astype(jnp.float32), approximate=False)
        return jnp.einsum('thk,thkd->td', (g * act).astype(xc.dtype), ve)

    return lax.map(chunk, xp).reshape(-1, D_MODEL)[:n]


def ffn_block(h, g, w_q, sub_keys, u, v):
    B, T, D = h.shape
    return h + peer_ffn(rms_norm(h, g).reshape(B * T, D), w_q, sub_keys, u, v).reshape(B, T, D)


LOWEST = -3.0e38
PEER_PAIRS = PEER_HEADS * PEER_TOPK
V7X_VMEM_BYTES = 64 * 1024 * 1024
SUBLANES = 8


def _take_max(work, lane, n_lanes):
    mx = jnp.max(work, axis=-1, keepdims=True)
    first = jnp.min(jnp.where(work == mx, lane, n_lanes), axis=-1, keepdims=True)
    return mx, first, jnp.where(lane == first, LOWEST, work)


TOPK_ROWS = 512


def _peer_topk_body(s1_ref, s2_ref, e_ref, g_ref):
    K, NK = PEER_TOPK, PEER_KEYS
    tm = s1_ref.shape[0]
    lane = lax.broadcasted_iota(jnp.int32, (1, NK), 1)
    lane_c = lax.broadcasted_iota(jnp.int32, (1, K * K), 1)
    lane_k = lax.broadcasted_iota(jnp.int32, (1, K), 1)
    row_of, col_of = lane_c // K, lane_c % K

    def rows(i, carry):
        r0 = pl.multiple_of(i * TOPK_ROWS, TOPK_ROWS)
        w1 = s1_ref[pl.ds(r0, TOPK_ROWS), :]
        w2 = s2_ref[pl.ds(r0, TOPK_ROWS), :]
        v1 = jnp.zeros((TOPK_ROWS, K * K), F32)
        v2 = jnp.zeros((TOPK_ROWS, K * K), F32)
        i1 = jnp.zeros((TOPK_ROWS, K * K), jnp.int32)
        i2 = jnp.zeros((TOPK_ROWS, K * K), jnp.int32)
        for k in range(K):
            m1, f1, w1 = _take_max(w1, lane, NK)
            m2, f2, w2 = _take_max(w2, lane, NK)
            v1 = jnp.where(row_of == k, m1, v1)
            i1 = jnp.where(row_of == k, f1, i1)
            v2 = jnp.where(col_of == k, m2, v2)
            i2 = jnp.where(col_of == k, f2, i2)
        cand = v1 + v2
        cidx = i1 * NK + i2
        sc = jnp.zeros((TOPK_ROWS, K), F32)
        ee = jnp.zeros((TOPK_ROWS, K), jnp.int32)
        for k in range(K):
            mc, fc, cand = _take_max(cand, lane_c, K * K)
            e_k = jnp.sum(jnp.where(lane_c == fc, cidx, 0), axis=-1, keepdims=True)
            sc = jnp.where(lane_k == k, mc, sc)
            ee = jnp.where(lane_k == k, e_k, ee)
        p = jnp.exp(sc - jnp.max(sc, axis=-1, keepdims=True))
        g_ref[pl.ds(r0, TOPK_ROWS), :] = p / jnp.sum(p, axis=-1, keepdims=True)
        e_ref[pl.ds(r0, TOPK_ROWS), :] = ee
        return carry

    lax.fori_loop(0, tm // TOPK_ROWS, rows, 0)


def peer_topk(s1, s2, *, tm=512):
    M = s1.shape[0]
    spec_in = pl.BlockSpec((tm, PEER_KEYS), lambda i: (i, 0))
    spec_out = pl.BlockSpec((tm, PEER_TOPK), lambda i: (i, 0))
    return pl.pallas_call(
        _peer_topk_body,
        grid=(M // tm,),
        in_specs=[spec_in, spec_in],
        out_specs=[spec_out, spec_out],
        out_shape=[jax.ShapeDtypeStruct((M, PEER_TOPK), jnp.int32),
                   jax.ShapeDtypeStruct((M, PEER_TOPK), F32)],
        compiler_params=pltpu.CompilerParams(dimension_semantics=("parallel",)),
        name="peer_topk",
    )(s1, s2)


def _peer_route_body(e_ref, g_ref, o_ref):
    tt = e_ref.shape[0]
    NK = PEER_KEYS
    sub = lax.broadcasted_iota(jnp.int32, (NK, PEER_PAIRS), 0)

    def one(t):
        e = e_ref[pl.ds(t, 1), :]
        g = g_ref[pl.ds(t, 1), :]
        rows = jnp.where((e // NK) == sub, g, 0.0).astype(BF16)
        cols = jnp.where((e % NK) == sub, 1.0, 0.0).astype(BF16)
        return lax.dot_general(rows, cols, (((1,), (1,)), ((), ())), preferred_element_type=F32)

    def group(i, carry):
        t0 = pl.multiple_of(i * SUBLANES, SUBLANES)
        g8 = jnp.stack([one(t0 + s) for s in range(SUBLANES)], axis=0)
        by_row = jnp.swapaxes(g8, 0, 1)
        for i1 in range(NK):
            o_ref[pl.ds(t0, SUBLANES), pl.ds(i1 * NK, NK)] = by_row[i1].astype(o_ref.dtype)
        return carry

    lax.fori_loop(0, tt // SUBLANES, group, 0)


def peer_route(eidx, gate, *, tt=256):
    N = eidx.shape[0]
    spec_in = pl.BlockSpec((tt, PEER_PAIRS), lambda i: (i, 0))
    return pl.pallas_call(
        _peer_route_body,
        grid=(N // tt,),
        in_specs=[spec_in, spec_in],
        out_specs=pl.BlockSpec((tt, PEER_EXPERTS), lambda i: (i, 0)),
        out_shape=jax.ShapeDtypeStruct((N, PEER_EXPERTS), BF16),
        compiler_params=pltpu.CompilerParams(dimension_semantics=("parallel",)),
        name="peer_route",
    )(eidx, gate)


def _gelu_exact(x):
    return 0.5 * x * (1.0 + lax.erf(x * (2.0 ** -0.5)))


def _peer_experts_body(x_ref, ut_ref, v_ref, g_ref, o_ref):
    c = pl.program_id(1)

    @pl.when(c == 0)
    def _():
        o_ref[...] = jnp.zeros(o_ref.shape, F32)

    h = jnp.dot(x_ref[...], ut_ref[...], preferred_element_type=F32)
    a = (g_ref[...].astype(F32) * _gelu_exact(h)).astype(BF16)
    o_ref[...] += jnp.dot(a, v_ref[...], preferred_element_type=F32)


def peer_experts(x, ut, v, gmat, *, tt=1024, ce=1024):
    N, D = x.shape
    E = ut.shape[1]
    return pl.pallas_call(
        _peer_experts_body,
        grid=(N // tt, E // ce),
        in_specs=[
            pl.BlockSpec((tt, D), lambda i, c: (i, 0)),
            pl.BlockSpec((D, ce), lambda i, c: (0, c)),
            pl.BlockSpec((ce, D), lambda i, c: (c, 0)),
            pl.BlockSpec((tt, ce), lambda i, c: (i, c)),
        ],
        out_specs=pl.BlockSpec((tt, D), lambda i, c: (i, 0)),
        out_shape=jax.ShapeDtypeStruct((N, D), F32),
        compiler_params=pltpu.CompilerParams(
            dimension_semantics=("parallel", "arbitrary"),
            vmem_limit_bytes=(V7X_VMEM_BYTES * 3) // 4),
        name="peer_experts",
    )(x, ut, v, gmat)


def peer_ffn_dense(x, w_q, sub_keys, u, v):
    n = x.shape[0]
    q = (x @ w_q).reshape(n, PEER_HEADS, 2, PEER_DK // 2)
    s1 = jnp.einsum('thd,kd->thk', q[:, :, 0], sub_keys[0]).astype(F32).reshape(n * PEER_HEADS, PEER_KEYS)
    s2 = jnp.einsum('thd,kd->thk', q[:, :, 1], sub_keys[1]).astype(F32).reshape(n * PEER_HEADS, PEER_KEYS)
    eidx, gate = peer_topk(s1, s2)
    gmat = peer_route(eidx.reshape(n, PEER_PAIRS), gate.reshape(n, PEER_PAIRS))
    return peer_experts(x.astype(BF16), u.astype(BF16).T, v.astype(BF16), gmat)


def _final_norm_body(x_ref, g_ref, o_ref):
    x = x_ref[...]
    y = x * lax.rsqrt(jnp.mean(x * x, axis=-1, keepdims=True) + RMS_EPS)
    o_ref[...] = y * g_ref[...]


def _final_norm(h, g):
    B, T, D = h.shape
    x = h.reshape(B * T, D)
    n = x.shape[0]
    tm = 512
    out = pl.pallas_call(
        _final_norm_body,
        grid=(n // tm,),
        in_specs=[pl.BlockSpec((tm, D), lambda i: (i, 0)), pl.BlockSpec((1, D), lambda i: (0, 0))],
        out_specs=pl.BlockSpec((tm, D), lambda i: (i, 0)),
        out_shape=jax.ShapeDtypeStruct((n, D), jnp.float32),
    )(x, g.reshape(1, D))
    return out.reshape(B, T, D)


def kernel(x_prompt, x_sample, cache_cmp_k, cache_cmp_v, cache_slc_k, cache_slc_v,
           cache_diff_k, cache_diff_v, cache_win_k, cache_win_v, cache_mem_k, cache_mem_v,
           page_table, mem_prompt, norm_mix, w_in, cmp_pe_k, cmp_pe_v, cmp_k_w1, cmp_k_w2,
           cmp_v_w1, cmp_v_w2, nsa_out_norm, diff_lq1, diff_lk1, diff_lq2, diff_lk2, diff_subln,
           w_out, norm_mem_q, norm_mem_src, w_mq, w_mk, w_mv, w_mo, norm_ffn, peer_wq, peer_keys,
           peer_u, peer_v, norm_final):
    f32 = jnp.float32
    hp, hs = x_prompt, x_sample
    pos_p = jnp.arange(x_prompt.shape[1])
    pos_s = PAST_LEN + jnp.arange(x_sample.shape[1])
    l = 0
    lam_init = 0.8 - 0.6 * math.exp(-0.3 * l)
    lam = (jnp.exp(jnp.sum(diff_lq1[l].astype(f32) * diff_lk1[l].astype(f32)))
           - jnp.exp(jnp.sum(diff_lq2[l].astype(f32) * diff_lk2[l].astype(f32))) + lam_init)
    cmp_w = (cmp_pe_k[l], cmp_k_w1[l], cmp_k_w2[l], cmp_pe_v[l], cmp_v_w1[l], cmp_v_w2[l])

    pp = project_in(rms_norm(hp, norm_mix[l]), w_in[l], pos_p)
    o_n, o_d = prompt_mixer(pp, lam, cmp_w)
    hp = hp + mix_out(o_n, o_d, nsa_out_norm[l], diff_subln[l], w_out[l], lam_init)
    mk, mv = mem_kv(mem_prompt, norm_mem_src[l], w_mk[l], w_mv[l])
    hp = mem_block(hp, norm_mem_q[l], w_mq[l], mk, mv, w_mo[l])
    Bp, Tp = x_prompt.shape[:2]
    w_p = min(WINDOW, Tp)

    ps = project_in(rms_norm(hs, norm_mix[l]), w_in[l], pos_s)
    o_n, o_d = sample_mixer(ps, page_table, cache_cmp_k[l], cache_cmp_v[l], cache_slc_k[l],
                            cache_slc_v[l], cache_diff_k[l], cache_diff_v[l], cache_win_k[l],
                            cache_win_v[l], lam, cmp_w)
    hs = hs + mix_out(o_n, o_d, nsa_out_norm[l], diff_subln[l], w_out[l], lam_init)
    hs = mem_block(hs, norm_mem_q[l], w_mq[l], cache_mem_k[l], cache_mem_v[l], w_mo[l])
    Bs, Ts = x_sample.shape[:2]

    h_all = jnp.concatenate([hp.reshape(Bp * Tp, D_MODEL), hs.reshape(Bs * Ts, D_MODEL)], axis=0)
    h_all = h_all + peer_ffn_dense(rms_norm(h_all, norm_ffn[l]), peer_wq[l], peer_keys[l],
                                   peer_u[l], peer_v[l])
    y_all = _final_norm(h_all[None], norm_final)[0]
    y_prompt = y_all[:Bp * Tp].reshape(Bp, Tp, D_MODEL)
    y_sample = y_all[Bp * Tp:].reshape(Bs, Ts, D_MODEL)
    st = lambda t: t[None]
    return (y_prompt, y_sample,
            st(pp[2]), st(pp[3]), st(pp[4]), st(pp[5]), st(pp[6][:, Tp - w_p:]), st(pp[7][:, Tp - w_p:]),
            st(pp[10].reshape(Bp, Tp, DIFF_HEADS, 2 * DIFF_DIM)), st(pp[11]), st(mk), st(mv),
            st(ps[2]), st(ps[3]), st(ps[4]), st(ps[5]),
            st(jnp.concatenate([cache_win_k[l], ps[6]], axis=1)[:, Ts:]),
            st(jnp.concatenate([cache_win_v[l], ps[7]], axis=1)[:, Ts:]),
            st(ps[10].reshape(Bs, Ts, DIFF_HEADS, 2 * DIFF_DIM)), st(ps[11]))
```

```python
import functools
import math
import jax, jax.numpy as jnp
from jax import lax
import numpy as np
from jax.experimental import pallas as pl
from jax.experimental.pallas import tpu as pltpu

D_MODEL = 1024
BATCH = 2
SEQ = 8192
DEPTH = 1
DEC_BATCH = 128
DEC_SEQ = 8
PAST_LEN = 8192
PAGE_SIZE = 128
HEAD_DIM = 64
NSA_HEADS = 8
NSA_KV_HEADS = 2
CMP_LEN = 32
CMP_STRIDE = 16
CMP_HIDDEN = 128
SEL_LEN = 64
SEL_TOPK = 16
WINDOW = 512
DIFF_HEADS = 4
DIFF_DIM = 64
ROPE_THETA = 500000.0
ROT_FRAC = 4
MEM_HEADS = 4
MEM_HEAD_DIM = 128
PEER_HEADS = 8
PEER_KEYS = 128
PEER_EXPERTS = PEER_KEYS * PEER_KEYS
PEER_DK = 256
PEER_TOPK = 16
PEER_CHUNK = 256
Q_BLOCK = 128
RMS_EPS = 1e-6
NEG = -1e30
BIG = 1e9

NSA_Q = NSA_HEADS * HEAD_DIM
NSA_KV = NSA_KV_HEADS * HEAD_DIM
NSA_GATE = NSA_HEADS * 3
DIFF_W = DIFF_HEADS * 2 * DIFF_DIM
IN_SPLITS = (NSA_Q, NSA_KV, NSA_KV, NSA_KV, NSA_KV, NSA_KV, NSA_KV, NSA_GATE, DIFF_W, DIFF_W, DIFF_W)
IN_WIDTH = sum(IN_SPLITS)
MIX_WIDTH = NSA_Q + DIFF_W
MEM_W = MEM_HEADS * MEM_HEAD_DIM
MEM_LEN = 256


def rms_norm(x, g):
    xf = x.astype(jnp.float32)
    y = xf * lax.rsqrt(jnp.mean(xf * xf, axis=-1, keepdims=True) + RMS_EPS)
    return (y * g.astype(jnp.float32)).astype(x.dtype)


def rope(x, pos):
    d = x.shape[-1]
    rot = d // ROT_FRAC
    half = rot // 2
    inv = jnp.power(jnp.float32(ROPE_THETA), -jnp.arange(half, dtype=jnp.float32) * 2.0 / rot)
    ang = pos.astype(jnp.float32)[:, None] * inv
    shp = (ang.shape[0],) + (1,) * (x.ndim - 3) + (half,)
    cos = jnp.cos(ang).reshape(shp)
    sin = jnp.sin(ang).reshape(shp)
    xf = x.astype(jnp.float32)
    x1, x2, rest = xf[..., :half], xf[..., half:rot], xf[..., rot:]
    out = jnp.concatenate([x1 * cos - x2 * sin, x2 * cos + x1 * sin, rest], axis=-1)
    return out.astype(x.dtype)


def masked_softmax(s, mask):
    s = jnp.where(mask, s.astype(jnp.float32), NEG)
    return jnp.where(mask, jax.nn.softmax(s, axis=-1), 0.0)


def compress_tokens(tok, pe, w1, w2):
    B, L, G, dh = tok.shape
    nc = (L - CMP_LEN) // CMP_STRIDE + 1
    idx = jnp.arange(nc)[:, None] * CMP_STRIDE + jnp.arange(CMP_LEN)[None, :]
    blk = tok[:, idx] + pe[:, None, :]
    blk = blk.transpose(0, 1, 3, 2, 4).reshape(B, nc, G, CMP_LEN * dh)
    return jax.nn.gelu(blk @ w1, approximate=False) @ w2


def to_blocks(t):
    B, L, G, dh = t.shape
    ns = -(-L // SEL_LEN)
    t = jnp.pad(t, ((0, 0), (0, ns * SEL_LEN - L), (0, 0), (0, 0)))
    return t.reshape(B, ns, SEL_LEN, G, dh).transpose(0, 3, 1, 2, 4)


def nsa_core(q, qr, q_pos, kc, vc, cmp_end, ksb, vsb, kw, vw, kw_pos, gates):
    B, Tq, H, dh = q.shape
    G = kc.shape[2]
    R = H // G
    scale = dh ** -0.5
    qg = q.reshape(B, Tq, G, R, dh)
    qrg = qr.reshape(B, Tq, G, R, dh)
    s = jnp.einsum('bqgrd,bngd->bqgrn', qg, kc) * scale
    m = (cmp_end[None, :] <= q_pos[:, None])[None, :, None, None, :]
    p_cmp = masked_softmax(s, m)
    o_cmp = jnp.einsum('bqgrn,bngd->bqgrd', p_cmp.astype(vc.dtype), vc)
    nc = kc.shape[1]
    ns = ksb.shape[2]
    sub = SEL_LEN // CMP_STRIDE
    n_sub = ns * sub
    imp = p_cmp.sum(axis=3)
    seg = sum(jnp.pad(imp, ((0, 0), (0, 0), (0, 0), (r, n_sub - nc - r)))
              for r in range(CMP_LEN // CMP_STRIDE))
    imp_blk = seg.reshape(B, Tq, G, ns, sub).sum(-1)
    blk = jnp.arange(ns)[None, :]
    cur = (q_pos // SEL_LEN)[:, None]
    forced = (blk == 0) | (blk == cur) | (blk == cur - 1)
    valid = blk <= cur
    score = jnp.where(forced[None, :, None, :], BIG, jnp.where(valid[None, :, None, :], imp_blk, -BIG))
    n_sel = min(SEL_TOPK, ns)
    _, idx = lax.top_k(score, n_sel)
    bi = jnp.arange(B)[:, None, None, None]
    gi = jnp.arange(G)[None, None, :, None]
    kg = ksb[bi, gi, idx].reshape(B, Tq, G, n_sel * SEL_LEN, dh)
    vg = vsb[bi, gi, idx].reshape(B, Tq, G, n_sel * SEL_LEN, dh)
    kpos = (idx[..., None] * SEL_LEN + jnp.arange(SEL_LEN)).reshape(B, Tq, G, n_sel * SEL_LEN)
    s = jnp.einsum('bqgrd,bqgmd->bqgrm', qrg, kg) * scale
    p = masked_softmax(s, (kpos <= q_pos[None, :, None, None])[:, :, :, None, :])
    o_slc = jnp.einsum('bqgrm,bqgmd->bqgrd', p.astype(vg.dtype), vg)
    s = jnp.einsum('bqgrd,bkgd->bqgrk', qrg, kw) * scale
    dist = q_pos[:, None] - kw_pos[None, :]
    m = ((dist >= 0) & (dist < WINDOW) & (kw_pos >= 0)[None, :])[None, :, None, None, :]
    p = masked_softmax(s, m)
    o_win = jnp.einsum('bqgrk,bkgd->bqgrd', p.astype(vw.dtype), vw)
    g = jax.nn.sigmoid(gates.astype(jnp.float32)).reshape(B, Tq, G, R, 3).astype(q.dtype)
    o = g[..., 0:1] * o_cmp + g[..., 1:2] * o_slc + g[..., 2:3] * o_win
    return o.reshape(B, Tq, H, dh)


def diff_core(qd, kd, vd, q_pos, k_pos, lam):
    s = jnp.einsum('bqhmd,bkhmd->bhmqk', qd, kd) * (DIFF_DIM ** -0.5)
    p = masked_softmax(s, k_pos[None, :] <= q_pos[:, None])
    a = p[:, :, 0] - lam * p[:, :, 1]
    return jnp.einsum('bhqk,bkhe->bqhe', a.astype(vd.dtype), vd)


def project_in(hn, w_in, pos):
    B, T, _ = hn.shape
    offs = [int(o) for o in np.cumsum(IN_SPLITS)[:-1]]
    q, kc, vc, ks, vs, kw, vw, gl, dq, dk, dv = jnp.split(hn @ w_in, offs, axis=-1)
    q = q.reshape(B, T, NSA_HEADS, HEAD_DIM)
    kv = lambda t: t.reshape(B, T, NSA_KV_HEADS, HEAD_DIM)
    dq = dq.reshape(B, T, DIFF_HEADS, 2, DIFF_DIM)
    dk = dk.reshape(B, T, DIFF_HEADS, 2, DIFF_DIM)
    return (q, rope(q, pos), kv(kc), kv(vc), rope(kv(ks), pos), kv(vs), rope(kv(kw), pos), kv(vw),
            gl.reshape(B, T, NSA_HEADS, 3), rope(dq, pos), rope(dk, pos),
            dv.reshape(B, T, DIFF_HEADS, 2 * DIFF_DIM))


BF16 = jnp.bfloat16
F32 = jnp.float32
GQA_REP = NSA_HEADS // NSA_KV_HEADS
NT_DIMS = (((1,), (1,)), ((), ()))


def _online_update(s, mask, v, m_ref, l_ref, acc_ref, v_t=False):
    m_old = m_ref[...]
    if mask is not None:
        s = jnp.where(mask, s, NEG)
    m_new = jnp.maximum(m_old, jnp.max(s, axis=-1, keepdims=True))
    p = jnp.exp(s - m_new)
    if mask is not None:
        p = jnp.where(mask, p, 0.0)
    alpha = jnp.exp(m_old - m_new)
    l_ref[...] = alpha * l_ref[...] + jnp.sum(p, axis=-1, keepdims=True)
    lead = p.shape[:-2]
    p2 = p.astype(BF16).reshape((-1, p.shape[-1]))
    if v_t:
        pv = lax.dot_general(p2, v, NT_DIMS, preferred_element_type=F32)
    else:
        pv = jnp.dot(p2, v, preferred_element_type=F32)
    acc_ref[...] = alpha * acc_ref[...] + pv.reshape(lead + (p.shape[-2], pv.shape[-1]))
    m_ref[...] = m_new


def _diff_body(lam_ref, q_ref, k_ref, v_ref, o_ref, m_sc, l_sc, acc_sc, *, tq):
    i = pl.program_id(2)
    j = pl.program_id(3)

    @pl.when(j == 0)
    def _():
        m_sc[...] = jnp.full(m_sc.shape, NEG, F32)
        l_sc[...] = jnp.zeros(l_sc.shape, F32)
        acc_sc[...] = jnp.zeros(acc_sc.shape, F32)

    def step(masked):
        v = v_ref[0]
        if masked:
            row = lax.broadcasted_iota(jnp.int32, (tq, tq), 0)
            col = lax.broadcasted_iota(jnp.int32, (tq, tq), 1)
            mask = col <= row
        else:
            mask = None
        for mp in range(2):
            s = lax.dot_general(q_ref[0, 0, mp], k_ref[0, 0, mp], (((1,), (1,)), ((), ())),
                                preferred_element_type=F32)
            _online_update(s, mask, v, m_sc.at[mp], l_sc.at[mp], acc_sc.at[mp])

    @pl.when(j < i)
    def _():
        step(False)

    @pl.when(j == i)
    def _():
        step(True)
        lam = lam_ref[0]
        o_ref[0] = acc_sc[0] / l_sc[0] - lam * (acc_sc[1] / l_sc[1])


def diff_attention_causal(dq, dk, dv, lam, *, tq=512):
    B, H, _, T, dd = dq.shape
    nq = T // tq
    body = lambda *a: _diff_body(*a, tq=tq)
    return pl.pallas_call(
        body,
        grid=(B, H, nq, nq),
        in_specs=[
            pl.BlockSpec(memory_space=pltpu.SMEM),
            pl.BlockSpec((1, 1, 2, tq, dd), lambda b, h, i, j: (b, h, 0, i, 0)),
            pl.BlockSpec((1, 1, 2, tq, dd), lambda b, h, i, j: (b, h, 0, jnp.minimum(j, i), 0)),
            pl.BlockSpec((1, tq, 2 * dd), lambda b, h, i, j: (b, jnp.minimum(j, i), h)),
        ],
        out_specs=pl.BlockSpec((1, tq, 2 * dd), lambda b, h, i, j: (b, i, h)),
        out_shape=jax.ShapeDtypeStruct((B, T, H * 2 * dd), F32),
        scratch_shapes=[pltpu.VMEM((2, tq, 1), F32), pltpu.VMEM((2, tq, 1), F32),
                        pltpu.VMEM((2, tq, 2 * dd), F32)],
        compiler_params=pltpu.CompilerParams(
            dimension_semantics=("parallel", "parallel", "parallel", "arbitrary")),
        name="diff_attention",
    )(lam.reshape(1).astype(F32), dq, dk, dv)


def _cmp_select_body(q_ref, kc_ref, vc_ref, segmat_ref, o_ref, sel_ref, *, tq, n_sel):
    i = pl.program_id(2)
    R = q_ref.shape[2]
    ncp = kc_ref.shape[2]
    ns = sel_ref.shape[3]
    q0 = i * tq
    q_pos = q0 + lax.broadcasted_iota(jnp.int32, (tq, 1), 0)
    cmp_end = lax.broadcasted_iota(jnp.int32, (1, ncp), 1) * CMP_STRIDE + (CMP_LEN - 1)
    mask = cmp_end <= q_pos
    kc = kc_ref[0, 0]
    vc = vc_ref[0, 0]
    imp = jnp.zeros((tq, ncp), F32)
    for r in range(R):
        s = lax.dot_general(q_ref[0, 0, r], kc, (((1,), (1,)), ((), ())), preferred_element_type=F32)
        s = jnp.where(mask, s, NEG)
        m = jnp.max(s, axis=-1, keepdims=True)
        p = jnp.where(mask, jnp.exp(s - m), 0.0)
        l = jnp.sum(p, axis=-1, keepdims=True)
        p = p * jnp.where(l > 0.0, 1.0 / l, 0.0)
        o_ref[0, 0, r] = jnp.dot(p.astype(BF16), vc, preferred_element_type=F32)
        imp = imp + p
    imp_blk = jnp.dot(imp, segmat_ref[...], preferred_element_type=F32,
                      precision=lax.Precision.HIGHEST)
    sel_ref[0, 0] = _select_blocks(imp_blk, q_pos, ns, n_sel).astype(sel_ref.dtype)


def _select_blocks(imp_blk, q_pos, n_blocks, n_sel):
    width = imp_blk.shape[-1]
    blk = lax.broadcasted_iota(jnp.int32, (1, width), 1)
    cur = q_pos // SEL_LEN
    forced = (blk == 0) | (blk == cur) | (blk == cur - 1)
    valid = blk <= cur
    work = jnp.where(forced, BIG, jnp.where(valid, imp_blk, -BIG))
    if width > n_blocks:
        work = jnp.where(blk < n_blocks, work, -3.0e38)
    sel = jnp.zeros(imp_blk.shape, F32)
    for _ in range(n_sel):
        mx = jnp.max(work, axis=-1, keepdims=True)
        first = jnp.min(jnp.where(work == mx, blk, width), axis=-1, keepdims=True)
        pick = blk == first
        sel = jnp.where(pick, 1.0, sel)
        work = jnp.where(pick, -3.0e38, work)
    return sel


def _segment_matrix(ncp, ns):
    sub = SEL_LEN // CMP_STRIDE
    n = np.arange(ncp)[:, None]
    j = np.arange(ns)[None, :]
    m = ((n >= sub * j) & (n < sub * j + sub)).astype(np.float32)
    for r in range(1, CMP_LEN // CMP_STRIDE):
        m = m + ((n + r >= sub * j) & (n + r < sub * j + sub)).astype(np.float32)
    return jnp.asarray(m, F32)


def nsa_cmp_select(q, kc, vc, ns, *, tq=256):
    B, G, R, T, dh = q.shape
    ncp = kc.shape[2]
    n_sel = min(SEL_TOPK, ns)
    body = lambda *a: _cmp_select_body(*a, tq=tq, n_sel=n_sel)
    return pl.pallas_call(
        body,
        grid=(B, G, T // tq),
        in_specs=[
            pl.BlockSpec((1, 1, R, tq, dh), lambda b, g, i: (b, g, 0, i, 0)),
            pl.BlockSpec((1, 1, ncp, dh), lambda b, g, i: (b, g, 0, 0)),
            pl.BlockSpec((1, 1, ncp, dh), lambda b, g, i: (b, g, 0, 0)),
            pl.BlockSpec((ncp, ns), lambda b, g, i: (0, 0)),
        ],
        out_specs=[
            pl.BlockSpec((1, 1, R, tq, dh), lambda b, g, i: (b, g, 0, i, 0)),
            pl.BlockSpec((1, 1, tq, ns), lambda b, g, i: (b, g, i, 0)),
        ],
        out_shape=[jax.ShapeDtypeStruct((B, G, R, T, dh), F32),
                   jax.ShapeDtypeStruct((B, G, T, ns), BF16)],
        compiler_params=pltpu.CompilerParams(dimension_semantics=("parallel", "parallel", "parallel")),
        name="nsa_cmp_select",
    )(q, kc, vc, _segment_matrix(ncp, ns))


def _slc_win_body(q_ref, ks_ref, vs_ref, kw_ref, vw_ref, sel_ref, exp_ref, os_ref, ow_ref,
                  ms_sc, ls_sc, as_sc, mw_sc, lw_sc, aw_sc, *, tq, tk):
    i = pl.program_id(2)
    j = pl.program_id(3)
    R = q_ref.shape[2]
    dh = q_ref.shape[4]
    q0 = i * tq
    last = (q0 + tq - 1) // tk
    first_win = jnp.maximum((q0 - (WINDOW - 1)) // tk, 0)

    @pl.when(j == 0)
    def _():
        for m_sc, l_sc, a_sc in ((ms_sc, ls_sc, as_sc), (mw_sc, lw_sc, aw_sc)):
            m_sc[...] = jnp.full(m_sc.shape, NEG, F32)
            l_sc[...] = jnp.zeros(l_sc.shape, F32)
            a_sc[...] = jnp.zeros(a_sc.shape, F32)

    def positions():
        q_pos = q0 + lax.broadcasted_iota(jnp.int32, (tq, tk), 0)
        k_pos = j * tk + lax.broadcasted_iota(jnp.int32, (tq, tk), 1)
        return q_pos, k_pos

    q = q_ref[0, 0].reshape(R * tq, dh)

    def scores(k_ref):
        s = lax.dot_general(q, k_ref[0, 0], (((1,), (1,)), ((), ())), preferred_element_type=F32)
        return s.reshape(R, tq, tk)

    def slc_step(diagonal):
        picked = jnp.dot(sel_ref[0, 0], exp_ref[...], preferred_element_type=F32) > 0.5
        if diagonal:
            q_pos, k_pos = positions()
            picked = picked & (k_pos <= q_pos)
        _online_update(scores(ks_ref), picked[None], vs_ref[0, 0], ms_sc, ls_sc, as_sc)

    def win_step():
        q_pos, k_pos = positions()
        dist = q_pos - k_pos
        mask = (dist >= 0) & (dist < WINDOW)
        _online_update(scores(kw_ref), mask[None], vw_ref[0, 0], mw_sc, lw_sc, aw_sc)

    @pl.when(j < last)
    def _():
        slc_step(False)

    @pl.when((j >= first_win) & (j <= last))
    def _():
        win_step()

    @pl.when(j == last)
    def _():
        slc_step(True)
        os_ref[0, 0] = as_sc[...] / ls_sc[...]
        ow_ref[0, 0] = aw_sc[...] / lw_sc[...]


def nsa_slc_win(qr, ks, vs, kw, vw, sel, *, tq=256, tk=512):
    B, G, R, T, dh = qr.shape
    ns = sel.shape[3]
    nk = T // tk
    expand = (jnp.arange(T)[None, :] // SEL_LEN == jnp.arange(ns)[:, None]).astype(BF16)
    last = lambda i: (i * tq + tq - 1) // tk
    kidx = lambda b, g, i, j: (b, g, jnp.minimum(j, last(i)), 0)
    widx = lambda b, g, i, j: (b, g, jnp.clip(j, jnp.maximum((i * tq - (WINDOW - 1)) // tk, 0), last(i)), 0)
    body = lambda *a: _slc_win_body(*a, tq=tq, tk=tk)
    o_spec = pl.BlockSpec((1, 1, R, tq, dh), lambda b, g, i, j: (b, g, 0, i, 0))
    return pl.pallas_call(
        body,
        grid=(B, G, T // tq, nk),
        in_specs=[
            pl.BlockSpec((1, 1, R, tq, dh), lambda b, g, i, j: (b, g, 0, i, 0)),
            pl.BlockSpec((1, 1, tk, dh), kidx),
            pl.BlockSpec((1, 1, tk, dh), kidx),
            pl.BlockSpec((1, 1, tk, dh), widx),
            pl.BlockSpec((1, 1, tk, dh), widx),
            pl.BlockSpec((1, 1, tq, ns), lambda b, g, i, j: (b, g, i, 0)),
            pl.BlockSpec((ns, tk), lambda b, g, i, j: (0, jnp.minimum(j, last(i)))),
        ],
        out_specs=[o_spec, o_spec],
        out_shape=[jax.ShapeDtypeStruct((B, G, R, T, dh), F32)] * 2,
        scratch_shapes=[pltpu.VMEM((R, tq, 1), F32), pltpu.VMEM((R, tq, 1), F32), pltpu.VMEM((R, tq, dh), F32),
                        pltpu.VMEM((R, tq, 1), F32), pltpu.VMEM((R, tq, 1), F32), pltpu.VMEM((R, tq, dh), F32)],
        compiler_params=pltpu.CompilerParams(
            dimension_semantics=("parallel", "parallel", "parallel", "arbitrary")),
        name="nsa_slc_win",
    )(qr, ks, vs, kw, vw, sel, expand)


def _heads_major(t, scale=None):
    B, T, H, dh = t.shape
    if scale is not None:
        t = t * scale
    return t.astype(BF16).reshape(B, T, NSA_KV_HEADS, H // NSA_KV_HEADS, dh).transpose(0, 2, 3, 1, 4)


def _groups_major(t):
    return t.astype(BF16).transpose(0, 2, 1, 3)


def prompt_mixer(pp, lam, cmp_w):
    q, qr, kc_t, vc_t, ks, vs, kw, vw, gates, dq, dk, dv = pp
    B, T = q.shape[:2]
    pe_k, wk1, wk2, pe_v, wv1, wv2 = cmp_w
    kc = compress_tokens(kc_t, pe_k, wk1, wk2)
    vc = compress_tokens(vc_t, pe_v, wv1, wv2)
    nc = kc.shape[1]
    ns = -(-T // SEL_LEN)
    ncp = ns * (SEL_LEN // CMP_STRIDE)
    pad_c = lambda t: _groups_major(jnp.pad(t, ((0, 0), (0, ncp - nc), (0, 0), (0, 0))))
    scale = HEAD_DIM ** -0.5
    o_cmp, sel = nsa_cmp_select(_heads_major(q, scale), pad_c(kc), pad_c(vc), ns)
    o_slc, o_win = nsa_slc_win(_heads_major(qr, scale), _groups_major(ks), _groups_major(vs),
                               _groups_major(kw), _groups_major(vw), sel)
    back = lambda t: t.transpose(0, 3, 1, 2, 4)
    g = jax.nn.sigmoid(gates.astype(F32)).reshape(B, T, NSA_KV_HEADS, GQA_REP, 3)
    o_n = (g[..., 0:1] * back(o_cmp) + g[..., 1:2] * back(o_slc) + g[..., 2:3] * back(o_win))
    o_n = o_n.reshape(B, T, NSA_HEADS, HEAD_DIM)

    dscale = DIFF_DIM ** -0.5
    dqh = (dq * dscale).astype(BF16).transpose(0, 2, 3, 1, 4)
    dkh = dk.astype(BF16).transpose(0, 2, 3, 1, 4)
    o_d = diff_attention_causal(dqh, dkh, dv.reshape(B, T, DIFF_W).astype(BF16), lam)
    return o_n, o_d.reshape(B, T, DIFF_HEADS, 2 * DIFF_DIM)


def sample_mixer(ps, page_table, c_cmp_k, c_cmp_v, c_slc_k, c_slc_v, c_diff_k, c_diff_v,
                 c_win_k, c_win_v, lam, cmp_w):
    q, qr, kc_t, vc_t, ks, vs, kw, vw, gates, dq, dk, dv = ps
    Bd, T = q.shape[:2]
    G, R = NSA_KV_HEADS, GQA_REP
    L = PAST_LEN + T
    n_pages = PAST_LEN // PAGE_SIZE
    chunks_per_page = PAGE_SIZE // CMP_STRIDE
    n_chunks = PAST_LEN // CMP_STRIDE
    nc = (L - CMP_LEN) // CMP_STRIDE + 1
    ns = -(-L // SEL_LEN)
    assert nc == n_chunks - 1 and T <= PAGE_SIZE and PAST_LEN % SEL_LEN == 0 and CMP_LEN == 2 * CMP_STRIDE
    pe_k, wk1, wk2, pe_v, wv1, wv2 = cmp_w

    pq_k = _chunk_hidden(_feature_major(c_cmp_k), wk1)
    pq_v = _chunk_hidden(_feature_major(c_cmp_v), wv1)
    bias_k = (pe_k.reshape(1, -1) @ wk1).astype(F32)
    bias_v = (pe_v.reshape(1, -1) @ wv1).astype(F32)

    rows = lambda t, n: t.reshape(Bd, T, n, -1).transpose(0, 2, 1, 3)
    q_rows = (rows(q, NSA_HEADS) * HEAD_DIM ** -0.5).astype(BF16).reshape(Bd, G * R * T, HEAD_DIM)
    o_cmp, sel = _sample_cmp_select(page_table, pq_k, pq_v, bias_k, bias_v, wk2.astype(BF16),
                                    wv2.astype(BF16), q_rows, ns, nc)

    def block_diag(t, n):
        d = t.shape[-1]
        eye = jnp.eye(n, dtype=t.dtype)[None, :, None, :, None]
        return (t[:, :, :, None, :] * eye).reshape(Bd, n * t.shape[2], n * d)

    qs_rows = (rows(qr, NSA_HEADS) * HEAD_DIM ** -0.5).reshape(Bd, G, R * T, HEAD_DIM)
    qs_blk = block_diag(qs_rows, G).astype(BF16)
    qd_rows = (dq * DIFF_DIM ** -0.5).transpose(0, 2, 3, 1, 4)
    eye2 = jnp.eye(2, dtype=F32)[None, None, :, None, :, None]
    qd_blk = (qd_rows[:, :, :, :, None, :] * eye2).reshape(Bd, DIFF_HEADS * 2 * T, 2 * DIFF_DIM).astype(BF16)
    sel_rows = jnp.broadcast_to(sel.reshape(Bd, G, 1, T, -1), (Bd, G, R, T, sel.shape[-1]))
    sel_rows = sel_rows.reshape(Bd, G * R * T, -1)
    head_rows = lambda c: c.reshape(c.shape[0], -1, c.shape[-1])
    new_head_rows = lambda t: jnp.pad(t.reshape(Bd, T * DIFF_HEADS, 2 * DIFF_DIM),
                                      ((0, 0), (0, PAGE_SIZE - T * DIFF_HEADS), (0, 0))).astype(BF16)
    new_t = lambda t: jnp.pad(t.reshape(Bd, T, -1).transpose(0, 2, 1),
                              ((0, 0), (0, 0), (0, PAGE_SIZE - T))).astype(BF16)
    assert T * DIFF_HEADS <= PAGE_SIZE
    od, os_, ow = _sample_attend(page_table, head_rows(c_diff_k), head_rows(c_diff_v),
                                 _feature_major(c_slc_k), _feature_major(c_slc_v),
                                 qd_blk, qs_blk, sel_rows, new_head_rows(dk), new_head_rows(dv), new_t(ks),
                                 new_t(vs), new_t(kw), new_t(vw), _feature_major(c_win_k), _feature_major(c_win_v))

    own = lambda t, n: jnp.einsum('bnxnd->bnxd', t.reshape(Bd, n, -1, n, t.shape[-1] // n))
    o_slc = own(os_, G).reshape(Bd, G, R, T, HEAD_DIM)
    o_win = own(ow, G).reshape(Bd, G, R, T, HEAD_DIM)
    o_cmp = o_cmp.reshape(Bd, G, R, T, HEAD_DIM)
    back = lambda t: t.transpose(0, 3, 1, 2, 4)
    g = jax.nn.sigmoid(gates.astype(F32)).reshape(Bd, T, G, R, 3)
    o_n = g[..., 0:1] * back(o_cmp) + g[..., 1:2] * back(o_slc) + g[..., 2:3] * back(o_win)
    od = od.reshape(Bd, DIFF_HEADS, 2, T, 2 * DIFF_DIM)
    o_d = (od[:, :, 0] - lam * od[:, :, 1]).transpose(0, 2, 1, 3)
    return o_n.reshape(Bd, T, NSA_HEADS, HEAD_DIM), o_d


def _feature_major(cache):
    return cache.transpose(0, 2, 3, 1).reshape(cache.shape[0], -1, cache.shape[1])


def _chunk_hidden_body(x_ref, w_ref, o_ref, rows_sc, *, pages):
    width = x_ref.shape[1]
    page = x_ref.shape[2]
    chunks = pages * (page // CMP_STRIDE)
    for p in range(pages):
        rows_sc[p * page:(p + 1) * page, :] = x_ref[p].T
    acc = jnp.zeros(o_ref.shape, F32)
    for t in range(CMP_STRIDE):
        lhs = rows_sc[pl.ds(t, chunks, stride=CMP_STRIDE), :].astype(BF16)
        acc = acc + jnp.dot(lhs, w_ref[t], preferred_element_type=F32)
    o_ref[...] = acc


def _chunk_hidden(cache_t, w1, *, pages=32):
    P, width, page = cache_t.shape
    dh = HEAD_DIM
    G = width // dh
    H = w1.shape[1]
    half = CMP_STRIDE * dh
    w_ab = jnp.concatenate([w1[:half].reshape(CMP_STRIDE, dh, H), w1[half:].reshape(CMP_STRIDE, dh, H)], axis=-1)
    eye = jnp.eye(G, dtype=w1.dtype)
    w_rows = (w_ab[:, None, :, None, :] * eye[None, :, None, :, None]).reshape(CMP_STRIDE, width, G * 2 * H)
    pages = math.gcd(P, pages)
    cpp = page // CMP_STRIDE
    return pl.pallas_call(
        functools.partial(_chunk_hidden_body, pages=pages),
        grid=(P // pages,),
        in_specs=[pl.BlockSpec((pages, width, page), lambda i: (i, 0, 0)),
                  pl.BlockSpec(w_rows.shape, lambda i: (0, 0, 0))],
        out_specs=pl.BlockSpec((pages * cpp, G * 2 * H), lambda i: (i, 0)),
        out_shape=jax.ShapeDtypeStruct((P * cpp, G * 2 * H), F32),
        scratch_shapes=[pltpu.VMEM((pages * page, width), F32)],
        compiler_params=pltpu.CompilerParams(dimension_semantics=("parallel",)),
        name="chunk_hidden",
    )(cache_t, w_rows.astype(BF16))


def _page_specs(block, pages_per_step):
    def index(b, j, pt_ref, *, s):
        return (pt_ref[b, j * pages_per_step + s],) + (0,) * len(block)
    return [pl.BlockSpec((1,) + block, functools.partial(index, s=s)) for s in range(pages_per_step)]


def _seq_spec(block):
    return pl.BlockSpec((1,) + block, lambda b, j, pt_ref: (b,) + (0,) * len(block))


def _const_spec(block):
    return pl.BlockSpec(block, lambda b, j, pt_ref: (0,) * len(block))


def _sample_cmp_body(pt_ref, *refs, pps, n_steps, n_real, n_blocks, n_sel):
    pqk, pqv = refs[:pps], refs[pps:2 * pps]
    bias_k, bias_v, w2k, w2v, q_ref, seg_ref, o_ref, sel_ref, pqk_sc, pqv_sc = refs[2 * pps:]
    j = pl.program_id(1)
    cpp = pqk[0].shape[1]
    for s in range(pps):
        row0 = pl.multiple_of((j * pps + s) * cpp, cpp)
        pqk_sc[pl.ds(row0, cpp), :] = pqk[s][0]
        pqv_sc[pl.ds(row0, cpp), :] = pqv[s][0]

    @pl.when(j == n_steps - 1)
    def _():
        n_rows = pqk_sc.shape[0]
        H = w2k.shape[0]
        G = pqk_sc.shape[1] // (2 * H)
        RT = q_ref.shape[1] // G
        T = sel_ref.shape[1] // G
        visible = lax.broadcasted_iota(jnp.int32, (1, n_rows), 1) < n_real
        q_pos = PAST_LEN + lax.broadcasted_iota(jnp.int32, (T, 1), 0)

        def compress(sc, bias_ref, w2_ref, g):
            first = sc[:, g * 2 * H:g * 2 * H + H]
            second = pltpu.roll(sc[:, g * 2 * H + H:(g + 1) * 2 * H], n_rows - 1, 0)
            hid = _gelu_exact(first + second + bias_ref[...])
            return jnp.dot(hid.astype(BF16), w2_ref[...], preferred_element_type=F32).astype(BF16)

        for g in range(G):
            kc = compress(pqk_sc, bias_k, w2k, g)
            vc = compress(pqv_sc, bias_v, w2v, g)
            s = lax.dot_general(q_ref[0, g * RT:(g + 1) * RT, :], kc, NT_DIMS, preferred_element_type=F32)
            s = jnp.where(visible, s, NEG)
            m = jnp.max(s, axis=-1, keepdims=True)
            p = jnp.where(visible, jnp.exp(s - m), 0.0)
            p = p / jnp.sum(p, axis=-1, keepdims=True)
            o_ref[0, g * RT:(g + 1) * RT, :] = jnp.dot(p.astype(BF16), vc, preferred_element_type=F32)
            imp = jnp.sum(p.reshape(RT // T, T, n_rows), axis=0)
            imp_blk = jnp.dot(imp, seg_ref[...], preferred_element_type=F32, precision=lax.Precision.HIGHEST)
            sel_ref[0, g * T:(g + 1) * T, :] = _select_blocks(imp_blk, q_pos, n_blocks, n_sel).astype(sel_ref.dtype)


def _sample_cmp_select(page_table, pq_k, pq_v, bias_k, bias_v, w2k, w2v, q_rows, ns, nc, *, pps=16):
    Bd, n_pages = page_table.shape
    cpp = PAGE_SIZE // CMP_STRIDE
    n_rows = n_pages * cpp
    GRT, dh = q_rows.shape[1:]
    G = NSA_KV_HEADS
    T = GRT // (G * GQA_REP)
    H = w2k.shape[0]
    width = pq_k.shape[1]
    ns_pad = -(-ns // 128) * 128
    pps = min(pps, n_pages)
    n_steps = n_pages // pps
    pq_k = pq_k.reshape(-1, cpp, width)
    pq_v = pq_v.reshape(-1, cpp, width)
    body = functools.partial(_sample_cmp_body, pps=pps, n_steps=n_steps, n_real=nc, n_blocks=ns,
                             n_sel=min(SEL_TOPK, ns))
    grid_spec = pltpu.PrefetchScalarGridSpec(
        num_scalar_prefetch=1,
        grid=(Bd, n_steps),
        in_specs=(_page_specs((cpp, width), pps) + _page_specs((cpp, width), pps)
                  + [_const_spec((1, H)), _const_spec((1, H)), _const_spec((H, dh)), _const_spec((H, dh)),
                     _seq_spec((GRT, dh)), _const_spec((n_rows, ns_pad))]),
        out_specs=[_seq_spec((GRT, dh)), _seq_spec((G * T, ns_pad))],
        scratch_shapes=[pltpu.VMEM((n_rows, width), F32), pltpu.VMEM((n_rows, width), F32)],
    )
    return pl.pallas_call(
        body,
        grid_spec=grid_spec,
        out_shape=[jax.ShapeDtypeStruct((Bd, GRT, dh), F32), jax.ShapeDtypeStruct((Bd, G * T, ns_pad), BF16)],
        compiler_params=pltpu.CompilerParams(dimension_semantics=("parallel", "arbitrary")),
        name="sample_cmp_select",
    )(page_table, *([pq_k] * pps), *([pq_v] * pps), bias_k, bias_v, w2k, w2v, q_rows,
      _segment_matrix(n_rows, ns_pad))


def _sample_attend_body(pt_ref, *refs, pps, n_steps, n_new, w_buf):
    dk, dv, ksr, vsr = (refs[i * pps:(i + 1) * pps] for i in range(4))
    (qd_ref, qs_ref, sel_ref, exp_ref, dkn_ref, dvn_ref, ksn_ref, vsn_ref, kwn_ref, vwn_ref, wk_ref, wv_ref,
     od_ref, os_ref, ow_ref, md, ld, ad, ms, ls, as_) = refs[4 * pps:]
    j = pl.program_id(1)
    page = PAGE_SIZE

    @pl.when(j == 0)
    def _():
        for m_sc, l_sc, a_sc in ((md, ld, ad), (ms, ls, as_)):
            m_sc[...] = jnp.full(m_sc.shape, NEG, F32)
            l_sc[...] = jnp.zeros(l_sc.shape, F32)
            a_sc[...] = jnp.zeros(a_sc.shape, F32)

    qd = qd_ref[0]
    qs = qs_ref[0]
    sel = sel_ref[0]
    n_rows = qd.shape[0]
    heads = dk[0].shape[1] // page
    bf = lambda r: r[0].astype(BF16)

    def own_head(width):
        r = lax.broadcasted_iota(jnp.int32, (n_rows, width), 0) // (n_rows // heads)
        c = lax.broadcasted_iota(jnp.int32, (n_rows, width), 1) % heads
        return r == c

    sd = jnp.concatenate([lax.dot_general(qd, bf(r), NT_DIMS, preferred_element_type=F32) for r in dk], axis=1)
    _online_update(sd, own_head(sd.shape[1]), jnp.concatenate([bf(r) for r in dv], axis=0), md, ld, ad)
    ss = jnp.dot(qs, jnp.concatenate([bf(r) for r in ksr], axis=1), preferred_element_type=F32)
    col0 = pl.multiple_of(j * (pps * page), pps * page)
    picked = jnp.dot(sel, exp_ref[:, pl.ds(col0, pps * page)], preferred_element_type=F32) > 0.5
    _online_update(ss, picked, jnp.concatenate([bf(r) for r in vsr], axis=1), ms, ls, as_, v_t=True)

    @pl.when(j == n_steps - 1)
    def _():
        t_idx = lax.broadcasted_iota(jnp.int32, (n_rows, page), 0) % n_new
        k_idx = lax.broadcasted_iota(jnp.int32, (n_rows, page), 1)
        causal = k_idx <= t_idx
        new_d = lax.dot_general(qd, dkn_ref[0], NT_DIMS, preferred_element_type=F32)
        _online_update(new_d, own_head(page) & ((k_idx // heads) <= t_idx), dvn_ref[0], md, ld, ad)
        od_ref[0] = ad[...] / ld[...]
        picked_new = (jnp.dot(sel, exp_ref[:, pl.ds(n_steps * pps * page, page)],
                              preferred_element_type=F32) > 0.5) & causal
        _online_update(jnp.dot(qs, ksn_ref[0], preferred_element_type=F32), picked_new, vsn_ref[0],
                       ms, ls, as_, v_t=True)
        os_ref[0] = as_[...] / ls[...]
        tb = lax.broadcasted_iota(jnp.int32, (n_rows, w_buf), 0) % n_new
        kb = lax.broadcasted_iota(jnp.int32, (n_rows, w_buf), 1)
        mask = jnp.concatenate([(w_buf + tb - kb) < WINDOW, causal], axis=1)
        k_all = jnp.concatenate([bf(wk_ref), kwn_ref[0]], axis=1)
        v_all = jnp.concatenate([bf(wv_ref), vwn_ref[0]], axis=1)
        s = jnp.where(mask, jnp.dot(qs, k_all, preferred_element_type=F32), NEG)
        m = jnp.max(s, axis=-1, keepdims=True)
        p = jnp.where(mask, jnp.exp(s - m), 0.0)
        pv = lax.dot_general(p.astype(BF16), v_all, NT_DIMS, preferred_element_type=F32)
        ow_ref[0] = pv / jnp.sum(p, axis=-1, keepdims=True)


def _sample_attend(page_table, c_dk, c_dv, c_ks, c_vs, qd_blk, qs_blk, sel_rows, dk_new, dv_new, ks_new, vs_new,
                   kw_new, vw_new, win_k, win_v, *, pps=8):
    Bd, n_pages = page_table.shape
    pps = min(pps, n_pages)
    n_steps = n_pages // pps
    w_buf = win_k.shape[2]
    dw, sw = c_dk.shape[2], c_ks.shape[1]
    drows = c_dk.shape[1]
    rd, rs = qd_blk.shape[1], qs_blk.shape[1]
    assert rd == rs
    ns_pad = sel_rows.shape[2]
    n_keys = (n_pages + 1) * PAGE_SIZE
    n_new = rs // NSA_HEADS
    expand = (jnp.arange(n_keys)[None, :] // SEL_LEN == jnp.arange(ns_pad)[:, None]).astype(BF16)
    body = functools.partial(_sample_attend_body, pps=pps, n_steps=n_steps, n_new=n_new, w_buf=w_buf)
    page = PAGE_SIZE
    grid_spec = pltpu.PrefetchScalarGridSpec(
        num_scalar_prefetch=1,
        grid=(Bd, n_steps),
        in_specs=(_page_specs((drows, dw), pps) + _page_specs((drows, dw), pps)
                  + _page_specs((sw, page), pps) + _page_specs((sw, page), pps)
                  + [_seq_spec((rd, dw)), _seq_spec((rs, sw)), _seq_spec((rs, ns_pad)),
                     _const_spec((ns_pad, n_keys)),
                     _seq_spec((page, dw)), _seq_spec((page, dw)), _seq_spec((sw, page)), _seq_spec((sw, page)),
                     _seq_spec((sw, page)), _seq_spec((sw, page)), _seq_spec((sw, w_buf)), _seq_spec((sw, w_buf))]),
        out_specs=[_seq_spec((rd, dw)), _seq_spec((rs, sw)), _seq_spec((rs, sw))],
        scratch_shapes=[pltpu.VMEM((rd, 1), F32), pltpu.VMEM((rd, 1), F32), pltpu.VMEM((rd, dw), F32),
                        pltpu.VMEM((rs, 1), F32), pltpu.VMEM((rs, 1), F32), pltpu.VMEM((rs, sw), F32)],
    )
    return pl.pallas_call(
        body,
        grid_spec=grid_spec,
        out_shape=[jax.ShapeDtypeStruct((Bd, rd, dw), F32), jax.ShapeDtypeStruct((Bd, rs, sw), F32),
                   jax.ShapeDtypeStruct((Bd, rs, sw), F32)],
        compiler_params=pltpu.CompilerParams(dimension_semantics=("parallel", "arbitrary"),
                                             vmem_limit_bytes=(V7X_VMEM_BYTES * 3) // 4),
        name="sample_attend",
    )(page_table, *([c_dk] * pps), *([c_dv] * pps), *([c_ks] * pps), *([c_vs] * pps),
      qd_blk, qs_blk, sel_rows, expand, dk_new, dv_new, ks_new, vs_new, kw_new, vw_new, win_k, win_v)


def mix_out(o_n, o_d, g_nsa, g_sub, w_out, lam_init):
    B, T = o_n.shape[:2]
    a = rms_norm(o_n, g_nsa.reshape(NSA_HEADS, HEAD_DIM))
    d = rms_norm(o_d, g_sub) * (1.0 - lam_init)
    return jnp.concatenate([a.reshape(B, T, NSA_Q), d.reshape(B, T, DIFF_W)], axis=-1) @ w_out


def mem_kv(mem, g_src, w_mk, w_mv):
    B, M, _ = mem.shape
    m = rms_norm(mem, g_src)
    return ((m @ w_mk).reshape(B, M, MEM_HEADS, MEM_HEAD_DIM),
            (m @ w_mv).reshape(B, M, MEM_HEADS, MEM_HEAD_DIM))


def mem_block(h, g, w_mq, mk, mv, w_mo):
    B, T, _ = h.shape
    q = (rms_norm(h, g) @ w_mq).reshape(B, T, MEM_HEADS, MEM_HEAD_DIM)
    s = jnp.einsum('bqhd,bmhd->bhqm', q, mk) * (MEM_HEAD_DIM ** -0.5)
    p = jax.nn.softmax(s.astype(jnp.float32), axis=-1)
    o = jnp.einsum('bhqm,bmhd->bqhd', p.astype(mv.dtype), mv)
    return h + o.reshape(B, T, MEM_W) @ w_mo


def peer_ffn(x, w_q, sub_keys, u, v):
    n = x.shape[0]
    pad = (-n) % PEER_CHUNK
    xp = jnp.pad(x, ((0, pad), (0, 0))).reshape(-1, PEER_CHUNK, D_MODEL)

    def chunk(xc):
        q = (xc @ w_q).reshape(PEER_CHUNK, PEER_HEADS, 2, PEER_DK // 2)
        s1 = jnp.einsum('thd,kd->thk', q[:, :, 0], sub_keys[0]).astype(jnp.float32)
        s2 = jnp.einsum('thd,kd->thk', q[:, :, 1], sub_keys[1]).astype(jnp.float32)
        v1, i1 = lax.top_k(s1, PEER_TOPK)
        v2, i2 = lax.top_k(s2, PEER_TOPK)
        cand = (v1[..., :, None] + v2[..., None, :]).reshape(PEER_CHUNK, PEER_HEADS, PEER_TOPK * PEER_TOPK)
        cidx = (i1[..., :, None] * PEER_KEYS + i2[..., None, :]).reshape(PEER_CHUNK, PEER_HEADS, PEER_TOPK * PEER_TOPK)
        sc, sel = lax.top_k(cand, PEER_TOPK)
        eidx = jnp.take_along_axis(cidx, sel, axis=-1)
        g = jax.nn.softmax(sc, axis=-1)
        ue = u[eidx]
        ve = v[eidx]
        act = jax.nn.gelu(jnp.einsum('td,thkd->thk', xc, ue).astype(jnp.float32), approximate=False)
        return jnp.einsum('thk,thkd->td', (g * act).astype(xc.dtype), ve)

    return lax.map(chunk, xp).reshape(-1, D_MODEL)[:n]


def ffn_block(h, g, w_q, sub_keys, u, v):
    B, T, D = h.shape
    return h + peer_ffn(rms_norm(h, g).reshape(B * T, D), w_q, sub_keys, u, v).reshape(B, T, D)


LOWEST = -3.0e38
PEER_PAIRS = PEER_HEADS * PEER_TOPK
V7X_VMEM_BYTES = 64 * 1024 * 1024
SUBLANES = 8


LANES = 128


def _take_max(work, ids, n_ids):
    mx = jnp.max(work, axis=0, keepdims=True)
    first = jnp.min(jnp.where(work == mx, ids, n_ids), axis=0, keepdims=True)
    return mx, first, jnp.where(ids == first, LOWEST, work)


def _top_sorted(scores, k):
    n = scores.shape[0]
    ids = lax.broadcasted_iota(jnp.int32, scores.shape, 0)
    vals, idxs = [], []
    for _ in range(k):
        mx, first, scores = _take_max(scores, ids, n)
        vals.append(mx)
        idxs.append(first)
    return jnp.concatenate(vals, axis=0), jnp.concatenate(idxs, axis=0)


def _peer_topk_body(s1_ref, s2_ref, e_ref, g_ref):
    K, NK = PEER_TOPK, PEER_KEYS
    width = s1_ref.shape[1]

    def lanes(i, carry):
        c0 = pl.multiple_of(i * LANES, LANES)
        v1, i1 = _top_sorted(s1_ref[:, pl.ds(c0, LANES)], K)
        v2, i2 = _top_sorted(s2_ref[:, pl.ds(c0, LANES)], K)
        cand = jnp.concatenate([v1[r:r + 1] + v2 for r in range(K)], axis=0)
        cidx = jnp.concatenate([i1[r:r + 1] * NK + i2 for r in range(K)], axis=0)
        cid = lax.broadcasted_iota(jnp.int32, cand.shape, 0)
        vals, experts = [], []
        for _ in range(K):
            mc, fc, cand = _take_max(cand, cid, K * K)
            vals.append(mc)
            experts.append(jnp.sum(jnp.where(cid == fc, cidx, 0), axis=0, keepdims=True))
        sc = jnp.concatenate(vals, axis=0)
        p = jnp.exp(sc - jnp.max(sc, axis=0, keepdims=True))
        g_ref[:, pl.ds(c0, LANES)] = p / jnp.sum(p, axis=0, keepdims=True)
        e_ref[:, pl.ds(c0, LANES)] = jnp.concatenate(experts, axis=0)
        return carry

    lax.fori_loop(0, width // LANES, lanes, 0)


def peer_topk(s1t, s2t, *, tm=1024):
    M = s1t.shape[1]
    tm = math.gcd(M, tm)
    spec_in = pl.BlockSpec((PEER_KEYS, tm), lambda i: (0, i))
    spec_out = pl.BlockSpec((PEER_TOPK, tm), lambda i: (0, i))
    return pl.pallas_call(
        _peer_topk_body,
        grid=(M // tm,),
        in_specs=[spec_in, spec_in],
        out_specs=[spec_out, spec_out],
        out_shape=[jax.ShapeDtypeStruct((PEER_TOPK, M), jnp.int32),
                   jax.ShapeDtypeStruct((PEER_TOPK, M), F32)],
        compiler_params=pltpu.CompilerParams(dimension_semantics=("parallel",)),
        name="peer_topk",
    )(s1t, s2t)


def _peer_route_body(e_ref, g_ref, o_ref):
    tt = e_ref.shape[0]
    NK = PEER_KEYS
    sub = lax.broadcasted_iota(jnp.int32, (NK, PEER_PAIRS), 0)

    def one(t):
        e = e_ref[pl.ds(t, 1), :]
        g = g_ref[pl.ds(t, 1), :]
        rows = jnp.where((e // NK) == sub, g, 0.0).astype(BF16)
        cols = jnp.where((e % NK) == sub, 1.0, 0.0).astype(BF16)
        return lax.dot_general(rows, cols, (((1,), (1,)), ((), ())), preferred_element_type=F32)

    def group(i, carry):
        t0 = pl.multiple_of(i * SUBLANES, SUBLANES)
        g8 = jnp.stack([one(t0 + s) for s in range(SUBLANES)], axis=0)
        by_row = jnp.swapaxes(g8, 0, 1)
        for i1 in range(NK):
            o_ref[pl.ds(t0, SUBLANES), pl.ds(i1 * NK, NK)] = by_row[i1].astype(o_ref.dtype)
        return carry

    lax.fori_loop(0, tt // SUBLANES, group, 0)


def peer_route(eidx, gate, *, tt=256):
    N = eidx.shape[0]
    spec_in = pl.BlockSpec((tt, PEER_PAIRS), lambda i: (i, 0))
    return pl.pallas_call(
        _peer_route_body,
        grid=(N // tt,),
        in_specs=[spec_in, spec_in],
        out_specs=pl.BlockSpec((tt, PEER_EXPERTS), lambda i: (i, 0)),
        out_shape=jax.ShapeDtypeStruct((N, PEER_EXPERTS), BF16),
        compiler_params=pltpu.CompilerParams(dimension_semantics=("parallel",)),
        name="peer_route",
    )(eidx, gate)


def _gelu_exact(x):
    return 0.5 * x * (1.0 + lax.erf(x * (2.0 ** -0.5)))


def _peer_experts_body(x_ref, ut_ref, v_ref, g_ref, o_ref):
    c = pl.program_id(1)

    @pl.when(c == 0)
    def _():
        o_ref[...] = jnp.zeros(o_ref.shape, F32)

    h = jnp.dot(x_ref[...], ut_ref[...], preferred_element_type=F32)
    a = (g_ref[...].astype(F32) * _gelu_exact(h)).astype(BF16)
    o_ref[...] += jnp.dot(a, v_ref[...], preferred_element_type=F32)


def peer_experts(x, ut, v, gmat, *, tt=1024, ce=1024):
    N, D = x.shape
    E = ut.shape[1]
    return pl.pallas_call(
        _peer_experts_body,
        grid=(N // tt, E // ce),
        in_specs=[
            pl.BlockSpec((tt, D), lambda i, c: (i, 0)),
            pl.BlockSpec((D, ce), lambda i, c: (0, c)),
            pl.BlockSpec((ce, D), lambda i, c: (c, 0)),
            pl.BlockSpec((tt, ce), lambda i, c: (i, c)),
        ],
        out_specs=pl.BlockSpec((tt, D), lambda i, c: (i, 0)),
        out_shape=jax.ShapeDtypeStruct((N, D), F32),
        compiler_params=pltpu.CompilerParams(
            dimension_semantics=("parallel", "arbitrary"),
            vmem_limit_bytes=(V7X_VMEM_BYTES * 3) // 4),
        name="peer_experts",
    )(x, ut, v, gmat)


def peer_ffn_dense(x, w_q, sub_keys, u, v):
    n = x.shape[0]
    q = (x @ w_q).reshape(n, PEER_HEADS, 2, PEER_DK // 2)
    s1t = jnp.einsum('kd,thd->kth', sub_keys[0], q[:, :, 0]).astype(F32).reshape(PEER_KEYS, n * PEER_HEADS)
    s2t = jnp.einsum('kd,thd->kth', sub_keys[1], q[:, :, 1]).astype(F32).reshape(PEER_KEYS, n * PEER_HEADS)
    eidx_t, gate_t = peer_topk(s1t, s2t)
    pairs = lambda t: t.reshape(PEER_TOPK, n, PEER_HEADS).transpose(1, 2, 0).reshape(n, PEER_PAIRS)
    gmat = peer_route(pairs(eidx_t), pairs(gate_t))
    return peer_experts(x.astype(BF16), u.astype(BF16).T, v.astype(BF16), gmat)


def _final_norm_body(x_ref, g_ref, o_ref):
    x = x_ref[...]
    y = x * lax.rsqrt(jnp.mean(x * x, axis=-1, keepdims=True) + RMS_EPS)
    o_ref[...] = y * g_ref[...]


def _final_norm(h, g):
    B, T, D = h.shape
    x = h.reshape(B * T, D)
    n = x.shape[0]
    tm = 512
    out = pl.pallas_call(
        _final_norm_body,
        grid=(n // tm,),
        in_specs=[pl.BlockSpec((tm, D), lambda i: (i, 0)), pl.BlockSpec((1, D), lambda i: (0, 0))],
        out_specs=pl.BlockSpec((tm, D), lambda i: (i, 0)),
        out_shape=jax.ShapeDtypeStruct((n, D), jnp.float32),
    )(x, g.reshape(1, D))
    return out.reshape(B, T, D)


def kernel(x_prompt, x_sample, cache_cmp_k, cache_cmp_v, cache_slc_k, cache_slc_v,
           cache_diff_k, cache_diff_v, cache_win_k, cache_win_v, cache_mem_k, cache_mem_v,
           page_table, mem_prompt, norm_mix, w_in, cmp_pe_k, cmp_pe_v, cmp_k_w1, cmp_k_w2,
           cmp_v_w1, cmp_v_w2, nsa_out_norm, diff_lq1, diff_lk1, diff_lq2, diff_lk2, diff_subln,
           w_out, norm_mem_q, norm_mem_src, w_mq, w_mk, w_mv, w_mo, norm_ffn, peer_wq, peer_keys,
           peer_u, peer_v, norm_final):
    f32 = jnp.float32
    hp, hs = x_prompt, x_sample
    pos_p = jnp.arange(x_prompt.shape[1])
    pos_s = PAST_LEN + jnp.arange(x_sample.shape[1])
    l = 0
    lam_init = 0.8 - 0.6 * math.exp(-0.3 * l)
    lam = (jnp.exp(jnp.sum(diff_lq1[l].astype(f32) * diff_lk1[l].astype(f32)))
           - jnp.exp(jnp.sum(diff_lq2[l].astype(f32) * diff_lk2[l].astype(f32))) + lam_init)
    cmp_w = (cmp_pe_k[l], cmp_k_w1[l], cmp_k_w2[l], cmp_pe_v[l], cmp_v_w1[l], cmp_v_w2[l])

    pp = project_in(rms_norm(hp, norm_mix[l]), w_in[l], pos_p)
    o_n, o_d = prompt_mixer(pp, lam, cmp_w)
    hp = hp + mix_out(o_n, o_d, nsa_out_norm[l], diff_subln[l], w_out[l], lam_init)
    mk, mv = mem_kv(mem_prompt, norm_mem_src[l], w_mk[l], w_mv[l])
    hp = mem_block(hp, norm_mem_q[l], w_mq[l], mk, mv, w_mo[l])
    Bp, Tp = x_prompt.shape[:2]
    w_p = min(WINDOW, Tp)

    ps = project_in(rms_norm(hs, norm_mix[l]), w_in[l], pos_s)
    o_n, o_d = sample_mixer(ps, page_table, cache_cmp_k[l], cache_cmp_v[l], cache_slc_k[l],
                            cache_slc_v[l], cache_diff_k[l], cache_diff_v[l], cache_win_k[l],
                            cache_win_v[l], lam, cmp_w)
    hs = hs + mix_out(o_n, o_d, nsa_out_norm[l], diff_subln[l], w_out[l], lam_init)
    hs = mem_block(hs, norm_mem_q[l], w_mq[l], cache_mem_k[l], cache_mem_v[l], w_mo[l])
    Bs, Ts = x_sample.shape[:2]

    h_all = jnp.concatenate([hp.reshape(Bp * Tp, D_MODEL), hs.reshape(Bs * Ts, D_MODEL)], axis=0)
    h_all = h_all + peer_ffn_dense(rms_norm(h_all, norm_ffn[l]), peer_wq[l], peer_keys[l],
                                   peer_u[l], peer_v[l])
    y_all = _final_norm(h_all[None], norm_final)[0]
    y_prompt = y_all[:Bp * Tp].reshape(Bp, Tp, D_MODEL)
    y_sample = y_all[Bp * Tp:].reshape(Bs, Ts, D_MODEL)
    st = lambda t: t[None]
    return (y_prompt, y_sample,
            st(pp[2]), st(pp[3]), st(pp[4]), st(pp[5]), st(pp[6][:, Tp - w_p:]), st(pp[7][:, Tp - w_p:]),
            st(pp[10].reshape(Bp, Tp, DIFF_HEADS, 2 * DIFF_DIM)), st(pp[11]), st(mk), st(mv),
            st(ps[2]), st(ps[3]), st(ps[4]), st(ps[5]),
            st(jnp.concatenate([cache_win_k[l], ps[6]], axis=1)[:, Ts:]),
            st(jnp.concatenate([cache_win_v[l], ps[7]], axis=1)[:, Ts:]),
            st(ps[10].reshape(Bs, Ts, DIFF_HEADS, 2 * DIFF_DIM)), st(ps[11]))
```

```python
import functools
import math
import jax, jax.numpy as jnp
from jax import lax
import numpy as np
from jax.experimental import pallas as pl
from jax.experimental.pallas import tpu as pltpu

D_MODEL = 1024
BATCH = 2
SEQ = 8192
DEPTH = 1
DEC_BATCH = 128
DEC_SEQ = 8
PAST_LEN = 8192
PAGE_SIZE = 128
HEAD_DIM = 64
NSA_HEADS = 8
NSA_KV_HEADS = 2
CMP_LEN = 32
CMP_STRIDE = 16
CMP_HIDDEN = 128
SEL_LEN = 64
SEL_TOPK = 16
WINDOW = 512
DIFF_HEADS = 4
DIFF_DIM = 64
ROPE_THETA = 500000.0
ROT_FRAC = 4
MEM_HEADS = 4
MEM_HEAD_DIM = 128
PEER_HEADS = 8
PEER_KEYS = 128
PEER_EXPERTS = PEER_KEYS * PEER_KEYS
PEER_DK = 256
PEER_TOPK = 16
PEER_CHUNK = 256
Q_BLOCK = 128
RMS_EPS = 1e-6
NEG = -1e30
BIG = 1e9

NSA_Q = NSA_HEADS * HEAD_DIM
NSA_KV = NSA_KV_HEADS * HEAD_DIM
NSA_GATE = NSA_HEADS * 3
DIFF_W = DIFF_HEADS * 2 * DIFF_DIM
IN_SPLITS = (NSA_Q, NSA_KV, NSA_KV, NSA_KV, NSA_KV, NSA_KV, NSA_KV, NSA_GATE, DIFF_W, DIFF_W, DIFF_W)
IN_WIDTH = sum(IN_SPLITS)
MIX_WIDTH = NSA_Q + DIFF_W
MEM_W = MEM_HEADS * MEM_HEAD_DIM
MEM_LEN = 256


def rms_norm(x, g):
    xf = x.astype(jnp.float32)
    y = xf * lax.rsqrt(jnp.mean(xf * xf, axis=-1, keepdims=True) + RMS_EPS)
    return (y * g.astype(jnp.float32)).astype(x.dtype)


def rope(x, pos):
    d = x.shape[-1]
    rot = d // ROT_FRAC
    half = rot // 2
    inv = jnp.power(jnp.float32(ROPE_THETA), -jnp.arange(half, dtype=jnp.float32) * 2.0 / rot)
    ang = pos.astype(jnp.float32)[:, None] * inv
    shp = (ang.shape[0],) + (1,) * (x.ndim - 3) + (half,)
    cos = jnp.cos(ang).reshape(shp)
    sin = jnp.sin(ang).reshape(shp)
    xf = x.astype(jnp.float32)
    x1, x2, rest = xf[..., :half], xf[..., half:rot], xf[..., rot:]
    out = jnp.concatenate([x1 * cos - x2 * sin, x2 * cos + x1 * sin, rest], axis=-1)
    return out.astype(x.dtype)


def masked_softmax(s, mask):
    s = jnp.where(mask, s.astype(jnp.float32), NEG)
    return jnp.where(mask, jax.nn.softmax(s, axis=-1), 0.0)


def compress_tokens(tok, pe, w1, w2):
    B, L, G, dh = tok.shape
    nc = (L - CMP_LEN) // CMP_STRIDE + 1
    idx = jnp.arange(nc)[:, None] * CMP_STRIDE + jnp.arange(CMP_LEN)[None, :]
    blk = tok[:, idx] + pe[:, None, :]
    blk = blk.transpose(0, 1, 3, 2, 4).reshape(B, nc, G, CMP_LEN * dh)
    return jax.nn.gelu(blk @ w1, approximate=False) @ w2


def to_blocks(t):
    B, L, G, dh = t.shape
    ns = -(-L // SEL_LEN)
    t = jnp.pad(t, ((0, 0), (0, ns * SEL_LEN - L), (0, 0), (0, 0)))
    return t.reshape(B, ns, SEL_LEN, G, dh).transpose(0, 3, 1, 2, 4)


def nsa_core(q, qr, q_pos, kc, vc, cmp_end, ksb, vsb, kw, vw, kw_pos, gates):
    B, Tq, H, dh = q.shape
    G = kc.shape[2]
    R = H // G
    scale = dh ** -0.5
    qg = q.reshape(B, Tq, G, R, dh)
    qrg = qr.reshape(B, Tq, G, R, dh)
    s = jnp.einsum('bqgrd,bngd->bqgrn', qg, kc) * scale
    m = (cmp_end[None, :] <= q_pos[:, None])[None, :, None, None, :]
    p_cmp = masked_softmax(s, m)
    o_cmp = jnp.einsum('bqgrn,bngd->bqgrd', p_cmp.astype(vc.dtype), vc)
    nc = kc.shape[1]
    ns = ksb.shape[2]
    sub = SEL_LEN // CMP_STRIDE
    n_sub = ns * sub
    imp = p_cmp.sum(axis=3)
    seg = sum(jnp.pad(imp, ((0, 0), (0, 0), (0, 0), (r, n_sub - nc - r)))
              for r in range(CMP_LEN // CMP_STRIDE))
    imp_blk = seg.reshape(B, Tq, G, ns, sub).sum(-1)
    blk = jnp.arange(ns)[None, :]
    cur = (q_pos // SEL_LEN)[:, None]
    forced = (blk == 0) | (blk == cur) | (blk == cur - 1)
    valid = blk <= cur
    score = jnp.where(forced[None, :, None, :], BIG, jnp.where(valid[None, :, None, :], imp_blk, -BIG))
    n_sel = min(SEL_TOPK, ns)
    _, idx = lax.top_k(score, n_sel)
    bi = jnp.arange(B)[:, None, None, None]
    gi = jnp.arange(G)[None, None, :, None]
    kg = ksb[bi, gi, idx].reshape(B, Tq, G, n_sel * SEL_LEN, dh)
    vg = vsb[bi, gi, idx].reshape(B, Tq, G, n_sel * SEL_LEN, dh)
    kpos = (idx[..., None] * SEL_LEN + jnp.arange(SEL_LEN)).reshape(B, Tq, G, n_sel * SEL_LEN)
    s = jnp.einsum('bqgrd,bqgmd->bqgrm', qrg, kg) * scale
    p = masked_softmax(s, (kpos <= q_pos[None, :, None, None])[:, :, :, None, :])
    o_slc = jnp.einsum('bqgrm,bqgmd->bqgrd', p.astype(vg.dtype), vg)
    s = jnp.einsum('bqgrd,bkgd->bqgrk', qrg, kw) * scale
    dist = q_pos[:, None] - kw_pos[None, :]
    m = ((dist >= 0) & (dist < WINDOW) & (kw_pos >= 0)[None, :])[None, :, None, None, :]
    p = masked_softmax(s, m)
    o_win = jnp.einsum('bqgrk,bkgd->bqgrd', p.astype(vw.dtype), vw)
    g = jax.nn.sigmoid(gates.astype(jnp.float32)).reshape(B, Tq, G, R, 3).astype(q.dtype)
    o = g[..., 0:1] * o_cmp + g[..., 1:2] * o_slc + g[..., 2:3] * o_win
    return o.reshape(B, Tq, H, dh)


def diff_core(qd, kd, vd, q_pos, k_pos, lam):
    s = jnp.einsum('bqhmd,bkhmd->bhmqk', qd, kd) * (DIFF_DIM ** -0.5)
    p = masked_softmax(s, k_pos[None, :] <= q_pos[:, None])
    a = p[:, :, 0] - lam * p[:, :, 1]
    return jnp.einsum('bhqk,bkhe->bqhe', a.astype(vd.dtype), vd)


def project_in(hn, w_in, pos):
    B, T, _ = hn.shape
    offs = [int(o) for o in np.cumsum(IN_SPLITS)[:-1]]
    q, kc, vc, ks, vs, kw, vw, gl, dq, dk, dv = jnp.split(hn @ w_in, offs, axis=-1)
    q = q.reshape(B, T, NSA_HEADS, HEAD_DIM)
    kv = lambda t: t.reshape(B, T, NSA_KV_HEADS, HEAD_DIM)
    dq = dq.reshape(B, T, DIFF_HEADS, 2, DIFF_DIM)
    dk = dk.reshape(B, T, DIFF_HEADS, 2, DIFF_DIM)
    return (q, rope(q, pos), kv(kc), kv(vc), rope(kv(ks), pos), kv(vs), rope(kv(kw), pos), kv(vw),
            gl.reshape(B, T, NSA_HEADS, 3), rope(dq, pos), rope(dk, pos),
            dv.reshape(B, T, DIFF_HEADS, 2 * DIFF_DIM))


BF16 = jnp.bfloat16
F32 = jnp.float32
GQA_REP = NSA_HEADS // NSA_KV_HEADS
NT_DIMS = (((1,), (1,)), ((), ()))


def _online_update(s, mask, v, m_ref, l_ref, acc_ref, v_t=False):
    m_old = m_ref[...]
    if mask is not None:
        s = jnp.where(mask, s, NEG)
    m_new = jnp.maximum(m_old, jnp.max(s, axis=-1, keepdims=True))
    p = jnp.exp(s - m_new)
    if mask is not None:
        p = jnp.where(mask, p, 0.0)
    alpha = jnp.exp(m_old - m_new)
    l_ref[...] = alpha * l_ref[...] + jnp.sum(p, axis=-1, keepdims=True)
    lead = p.shape[:-2]
    p2 = p.astype(BF16).reshape((-1, p.shape[-1]))
    if v_t:
        pv = lax.dot_general(p2, v, NT_DIMS, preferred_element_type=F32)
    else:
        pv = jnp.dot(p2, v, preferred_element_type=F32)
    acc_ref[...] = alpha * acc_ref[...] + pv.reshape(lead + (p.shape[-2], pv.shape[-1]))
    m_ref[...] = m_new


def _diff_body(lam_ref, q_ref, k_ref, v_ref, o_ref, m_sc, l_sc, acc_sc, *, tq):
    i = pl.program_id(2)
    j = pl.program_id(3)

    @pl.when(j == 0)
    def _():
        m_sc[...] = jnp.full(m_sc.shape, NEG, F32)
        l_sc[...] = jnp.zeros(l_sc.shape, F32)
        acc_sc[...] = jnp.zeros(acc_sc.shape, F32)

    def step(masked):
        v = v_ref[0]
        if masked:
            row = lax.broadcasted_iota(jnp.int32, (tq, tq), 0)
            col = lax.broadcasted_iota(jnp.int32, (tq, tq), 1)
            mask = col <= row
        else:
            mask = None
        for mp in range(2):
            s = lax.dot_general(q_ref[0, 0, mp], k_ref[0, 0, mp], (((1,), (1,)), ((), ())),
                                preferred_element_type=F32)
            _online_update(s, mask, v, m_sc.at[mp], l_sc.at[mp], acc_sc.at[mp])

    @pl.when(j < i)
    def _():
        step(False)

    @pl.when(j == i)
    def _():
        step(True)
        lam = lam_ref[0]
        o_ref[0] = acc_sc[0] / l_sc[0] - lam * (acc_sc[1] / l_sc[1])


def diff_attention_causal(dq, dk, dv, lam, *, tq=512):
    B, H, _, T, dd = dq.shape
    nq = T // tq
    body = lambda *a: _diff_body(*a, tq=tq)
    return pl.pallas_call(
        body,
        grid=(B, H, nq, nq),
        in_specs=[
            pl.BlockSpec(memory_space=pltpu.SMEM),
            pl.BlockSpec((1, 1, 2, tq, dd), lambda b, h, i, j: (b, h, 0, i, 0)),
            pl.BlockSpec((1, 1, 2, tq, dd), lambda b, h, i, j: (b, h, 0, jnp.minimum(j, i), 0)),
            pl.BlockSpec((1, tq, 2 * dd), lambda b, h, i, j: (b, jnp.minimum(j, i), h)),
        ],
        out_specs=pl.BlockSpec((1, tq, 2 * dd), lambda b, h, i, j: (b, i, h)),
        out_shape=jax.ShapeDtypeStruct((B, T, H * 2 * dd), F32),
        scratch_shapes=[pltpu.VMEM((2, tq, 1), F32), pltpu.VMEM((2, tq, 1), F32),
                        pltpu.VMEM((2, tq, 2 * dd), F32)],
        compiler_params=pltpu.CompilerParams(
            dimension_semantics=("parallel", "parallel", "parallel", "arbitrary")),
        name="diff_attention",
    )(lam.reshape(1).astype(F32), dq, dk, dv)


def _cmp_select_body(q_ref, kc_ref, vc_ref, segmat_ref, o_ref, sel_ref, *, tq, n_sel):
    i = pl.program_id(2)
    R = q_ref.shape[2]
    ncp = kc_ref.shape[2]
    ns = sel_ref.shape[3]
    q0 = i * tq
    q_pos = q0 + lax.broadcasted_iota(jnp.int32, (tq, 1), 0)
    cmp_end = lax.broadcasted_iota(jnp.int32, (1, ncp), 1) * CMP_STRIDE + (CMP_LEN - 1)
    mask = cmp_end <= q_pos
    kc = kc_ref[0, 0]
    vc = vc_ref[0, 0]
    imp = jnp.zeros((tq, ncp), F32)
    for r in range(R):
        s = lax.dot_general(q_ref[0, 0, r], kc, (((1,), (1,)), ((), ())), preferred_element_type=F32)
        s = jnp.where(mask, s, NEG)
        m = jnp.max(s, axis=-1, keepdims=True)
        p = jnp.where(mask, jnp.exp(s - m), 0.0)
        l = jnp.sum(p, axis=-1, keepdims=True)
        p = p * jnp.where(l > 0.0, 1.0 / l, 0.0)
        o_ref[0, 0, r] = jnp.dot(p.astype(BF16), vc, preferred_element_type=F32)
        imp = imp + p
    imp_blk = jnp.dot(imp, segmat_ref[...], preferred_element_type=F32,
                      precision=lax.Precision.HIGHEST)
    sel_ref[0, 0] = _select_blocks(imp_blk, q_pos, ns, n_sel).astype(sel_ref.dtype)


def _select_blocks(imp_blk, q_pos, n_blocks, n_sel):
    width = imp_blk.shape[-1]
    blk = lax.broadcasted_iota(jnp.int32, (1, width), 1)
    cur = q_pos // SEL_LEN
    forced = (blk == 0) | (blk == cur) | (blk == cur - 1)
    valid = blk <= cur
    work = jnp.where(forced, BIG, jnp.where(valid, imp_blk, -BIG))
    if width > n_blocks:
        work = jnp.where(blk < n_blocks, work, -3.0e38)
    sel = jnp.zeros(imp_blk.shape, F32)
    for _ in range(n_sel):
        mx = jnp.max(work, axis=-1, keepdims=True)
        first = jnp.min(jnp.where(work == mx, blk, width), axis=-1, keepdims=True)
        pick = blk == first
        sel = jnp.where(pick, 1.0, sel)
        work = jnp.where(pick, -3.0e38, work)
    return sel


def _segment_matrix(ncp, ns):
    sub = SEL_LEN // CMP_STRIDE
    n = np.arange(ncp)[:, None]
    j = np.arange(ns)[None, :]
    m = ((n >= sub * j) & (n < sub * j + sub)).astype(np.float32)
    for r in range(1, CMP_LEN // CMP_STRIDE):
        m = m + ((n + r >= sub * j) & (n + r < sub * j + sub)).astype(np.float32)
    return jnp.asarray(m, F32)


def nsa_cmp_select(q, kc, vc, ns, *, tq=256):
    B, G, R, T, dh = q.shape
    ncp = kc.shape[2]
    n_sel = min(SEL_TOPK, ns)
    body = lambda *a: _cmp_select_body(*a, tq=tq, n_sel=n_sel)
    return pl.pallas_call(
        body,
        grid=(B, G, T // tq),
        in_specs=[
            pl.BlockSpec((1, 1, R, tq, dh), lambda b, g, i: (b, g, 0, i, 0)),
            pl.BlockSpec((1, 1, ncp, dh), lambda b, g, i: (b, g, 0, 0)),
            pl.BlockSpec((1, 1, ncp, dh), lambda b, g, i: (b, g, 0, 0)),
            pl.BlockSpec((ncp, ns), lambda b, g, i: (0, 0)),
        ],
        out_specs=[
            pl.BlockSpec((1, 1, R, tq, dh), lambda b, g, i: (b, g, 0, i, 0)),
            pl.BlockSpec((1, 1, tq, ns), lambda b, g, i: (b, g, i, 0)),
        ],
        out_shape=[jax.ShapeDtypeStruct((B, G, R, T, dh), F32),
                   jax.ShapeDtypeStruct((B, G, T, ns), BF16)],
        compiler_params=pltpu.CompilerParams(dimension_semantics=("parallel", "parallel", "parallel")),
        name="nsa_cmp_select",
    )(q, kc, vc, _segment_matrix(ncp, ns))


def _slc_win_body(q_ref, ks_ref, vs_ref, kw_ref, vw_ref, sel_ref, exp_ref, os_ref, ow_ref,
                  ms_sc, ls_sc, as_sc, mw_sc, lw_sc, aw_sc, *, tq, tk):
    i = pl.program_id(2)
    j = pl.program_id(3)
    R = q_ref.shape[2]
    dh = q_ref.shape[4]
    q0 = i * tq
    last = (q0 + tq - 1) // tk
    first_win = jnp.maximum((q0 - (WINDOW - 1)) // tk, 0)

    @pl.when(j == 0)
    def _():
        for m_sc, l_sc, a_sc in ((ms_sc, ls_sc, as_sc), (mw_sc, lw_sc, aw_sc)):
            m_sc[...] = jnp.full(m_sc.shape, NEG, F32)
            l_sc[...] = jnp.zeros(l_sc.shape, F32)
            a_sc[...] = jnp.zeros(a_sc.shape, F32)

    def positions():
        q_pos = q0 + lax.broadcasted_iota(jnp.int32, (tq, tk), 0)
        k_pos = j * tk + lax.broadcasted_iota(jnp.int32, (tq, tk), 1)
        return q_pos, k_pos

    q = q_ref[0, 0].reshape(R * tq, dh)

    def scores(k_ref):
        s = lax.dot_general(q, k_ref[0, 0], (((1,), (1,)), ((), ())), preferred_element_type=F32)
        return s.reshape(R, tq, tk)

    def slc_step(diagonal):
        picked = jnp.dot(sel_ref[0, 0], exp_ref[...], preferred_element_type=F32) > 0.5
        if diagonal:
            q_pos, k_pos = positions()
            picked = picked & (k_pos <= q_pos)
        _online_update(scores(ks_ref), picked[None], vs_ref[0, 0], ms_sc, ls_sc, as_sc)

    def win_step():
        q_pos, k_pos = positions()
        dist = q_pos - k_pos
        mask = (dist >= 0) & (dist < WINDOW)
        _online_update(scores(kw_ref), mask[None], vw_ref[0, 0], mw_sc, lw_sc, aw_sc)

    @pl.when(j < last)
    def _():
        slc_step(False)

    @pl.when((j >= first_win) & (j <= last))
    def _():
        win_step()

    @pl.when(j == last)
    def _():
        slc_step(True)
        os_ref[0, 0] = as_sc[...] / ls_sc[...]
        ow_ref[0, 0] = aw_sc[...] / lw_sc[...]


def nsa_slc_win(qr, ks, vs, kw, vw, sel, *, tq=256, tk=1024):
    B, G, R, T, dh = qr.shape
    ns = sel.shape[3]
    nk = T // tk
    expand = (jnp.arange(T)[None, :] // SEL_LEN == jnp.arange(ns)[:, None]).astype(BF16)
    last = lambda i: (i * tq + tq - 1) // tk
    kidx = lambda b, g, i, j: (b, g, jnp.minimum(j, last(i)), 0)
    widx = lambda b, g, i, j: (b, g, jnp.clip(j, jnp.maximum((i * tq - (WINDOW - 1)) // tk, 0), last(i)), 0)
    body = lambda *a: _slc_win_body(*a, tq=tq, tk=tk)
    o_spec = pl.BlockSpec((1, 1, R, tq, dh), lambda b, g, i, j: (b, g, 0, i, 0))
    return pl.pallas_call(
        body,
        grid=(B, G, T // tq, nk),
        in_specs=[
            pl.BlockSpec((1, 1, R, tq, dh), lambda b, g, i, j: (b, g, 0, i, 0)),
            pl.BlockSpec((1, 1, tk, dh), kidx),
            pl.BlockSpec((1, 1, tk, dh), kidx),
            pl.BlockSpec((1, 1, tk, dh), widx),
            pl.BlockSpec((1, 1, tk, dh), widx),
            pl.BlockSpec((1, 1, tq, ns), lambda b, g, i, j: (b, g, i, 0)),
            pl.BlockSpec((ns, tk), lambda b, g, i, j: (0, jnp.minimum(j, last(i)))),
        ],
        out_specs=[o_spec, o_spec],
        out_shape=[jax.ShapeDtypeStruct((B, G, R, T, dh), F32)] * 2,
        scratch_shapes=[pltpu.VMEM((R, tq, 1), F32), pltpu.VMEM((R, tq, 1), F32), pltpu.VMEM((R, tq, dh), F32),
                        pltpu.VMEM((R, tq, 1), F32), pltpu.VMEM((R, tq, 1), F32), pltpu.VMEM((R, tq, dh), F32)],
        compiler_params=pltpu.CompilerParams(
            dimension_semantics=("parallel", "parallel", "parallel", "arbitrary")),
        name="nsa_slc_win",
    )(qr, ks, vs, kw, vw, sel, expand)


def _heads_major(t, scale=None):
    B, T, H, dh = t.shape
    if scale is not None:
        t = t * scale
    return t.astype(BF16).reshape(B, T, NSA_KV_HEADS, H // NSA_KV_HEADS, dh).transpose(0, 2, 3, 1, 4)


def _groups_major(t):
    return t.astype(BF16).transpose(0, 2, 1, 3)


def prompt_mixer(pp, lam, cmp_w):
    q, qr, kc_t, vc_t, ks, vs, kw, vw, gates, dq, dk, dv = pp
    B, T = q.shape[:2]
    pe_k, wk1, wk2, pe_v, wv1, wv2 = cmp_w
    kc = compress_tokens(kc_t, pe_k, wk1, wk2)
    vc = compress_tokens(vc_t, pe_v, wv1, wv2)
    nc = kc.shape[1]
    ns = -(-T // SEL_LEN)
    ncp = ns * (SEL_LEN // CMP_STRIDE)
    pad_c = lambda t: _groups_major(jnp.pad(t, ((0, 0), (0, ncp - nc), (0, 0), (0, 0))))
    scale = HEAD_DIM ** -0.5
    o_cmp, sel = nsa_cmp_select(_heads_major(q, scale), pad_c(kc), pad_c(vc), ns)
    o_slc, o_win = nsa_slc_win(_heads_major(qr, scale), _groups_major(ks), _groups_major(vs),
                               _groups_major(kw), _groups_major(vw), sel)
    back = lambda t: t.transpose(0, 3, 1, 2, 4)
    g = jax.nn.sigmoid(gates.astype(F32)).reshape(B, T, NSA_KV_HEADS, GQA_REP, 3)
    o_n = (g[..., 0:1] * back(o_cmp) + g[..., 1:2] * back(o_slc) + g[..., 2:3] * back(o_win))
    o_n = o_n.reshape(B, T, NSA_HEADS, HEAD_DIM)

    dscale = DIFF_DIM ** -0.5
    dqh = (dq * dscale).astype(BF16).transpose(0, 2, 3, 1, 4)
    dkh = dk.astype(BF16).transpose(0, 2, 3, 1, 4)
    o_d = diff_attention_causal(dqh, dkh, dv.reshape(B, T, DIFF_W).astype(BF16), lam)
    return o_n, o_d.reshape(B, T, DIFF_HEADS, 2 * DIFF_DIM)


def sample_mixer(ps, page_table, c_cmp_k, c_cmp_v, c_slc_k, c_slc_v, c_diff_k, c_diff_v,
                 c_win_k, c_win_v, lam, cmp_w):
    q, qr, kc_t, vc_t, ks, vs, kw, vw, gates, dq, dk, dv = ps
    Bd, T = q.shape[:2]
    G, R = NSA_KV_HEADS, GQA_REP
    L = PAST_LEN + T
    n_pages = PAST_LEN // PAGE_SIZE
    chunks_per_page = PAGE_SIZE // CMP_STRIDE
    n_chunks = PAST_LEN // CMP_STRIDE
    nc = (L - CMP_LEN) // CMP_STRIDE + 1
    ns = -(-L // SEL_LEN)
    assert nc == n_chunks - 1 and T <= PAGE_SIZE and PAST_LEN % SEL_LEN == 0 and CMP_LEN == 2 * CMP_STRIDE
    pe_k, wk1, wk2, pe_v, wv1, wv2 = cmp_w

    pq_k = _chunk_hidden(_feature_major(c_cmp_k), wk1)
    pq_v = _chunk_hidden(_feature_major(c_cmp_v), wv1)
    bias_k = (pe_k.reshape(1, -1) @ wk1).astype(F32)
    bias_v = (pe_v.reshape(1, -1) @ wv1).astype(F32)

    rows = lambda t, n: t.reshape(Bd, T, n, -1).transpose(0, 2, 1, 3)
    q_rows = (rows(q, NSA_HEADS) * HEAD_DIM ** -0.5).astype(BF16).reshape(Bd, G * R * T, HEAD_DIM)
    o_cmp, sel = _sample_cmp_select(page_table, pq_k, pq_v, bias_k, bias_v, wk2.astype(BF16),
                                    wv2.astype(BF16), q_rows, ns, nc)

    def block_diag(t, n):
        d = t.shape[-1]
        eye = jnp.eye(n, dtype=t.dtype)[None, :, None, :, None]
        return (t[:, :, :, None, :] * eye).reshape(Bd, n * t.shape[2], n * d)

    qs_rows = (rows(qr, NSA_HEADS) * HEAD_DIM ** -0.5).reshape(Bd, G, R * T, HEAD_DIM)
    qs_blk = block_diag(qs_rows, G).astype(BF16)
    qd_rows = (dq * DIFF_DIM ** -0.5).transpose(0, 2, 3, 1, 4)
    eye2 = jnp.eye(2, dtype=F32)[None, None, :, None, :, None]
    qd_blk = (qd_rows[:, :, :, :, None, :] * eye2).reshape(Bd, DIFF_HEADS * 2 * T, 2 * DIFF_DIM).astype(BF16)
    sel_rows = jnp.broadcast_to(sel.reshape(Bd, G, 1, T, -1), (Bd, G, R, T, sel.shape[-1]))
    sel_rows = sel_rows.reshape(Bd, G * R * T, -1)
    head_rows = lambda c: c.reshape(c.shape[0], -1, c.shape[-1])
    new_head_rows = lambda t: jnp.pad(t.reshape(Bd, T * DIFF_HEADS, 2 * DIFF_DIM),
                                      ((0, 0), (0, PAGE_SIZE - T * DIFF_HEADS), (0, 0))).astype(BF16)
    new_t = lambda t: jnp.pad(t.reshape(Bd, T, -1).transpose(0, 2, 1),
                              ((0, 0), (0, 0), (0, PAGE_SIZE - T))).astype(BF16)
    assert T * DIFF_HEADS <= PAGE_SIZE
    od, os_, ow = _sample_attend(page_table, head_rows(c_diff_k), head_rows(c_diff_v),
                                 _feature_major(c_slc_k), _feature_major(c_slc_v),
                                 qd_blk, qs_blk, sel_rows, new_head_rows(dk), new_head_rows(dv), new_t(ks),
                                 new_t(vs), new_t(kw), new_t(vw), _feature_major(c_win_k), _feature_major(c_win_v))

    own = lambda t, n: jnp.einsum('bnxnd->bnxd', t.reshape(Bd, n, -1, n, t.shape[-1] // n))
    o_slc = own(os_, G).reshape(Bd, G, R, T, HEAD_DIM)
    o_win = own(ow, G).reshape(Bd, G, R, T, HEAD_DIM)
    o_cmp = o_cmp.reshape(Bd, G, R, T, HEAD_DIM)
    back = lambda t: t.transpose(0, 3, 1, 2, 4)
    g = jax.nn.sigmoid(gates.astype(F32)).reshape(Bd, T, G, R, 3)
    o_n = g[..., 0:1] * back(o_cmp) + g[..., 1:2] * back(o_slc) + g[..., 2:3] * back(o_win)
    od = od.reshape(Bd, DIFF_HEADS, 2, T, 2 * DIFF_DIM)
    o_d = (od[:, :, 0] - lam * od[:, :, 1]).transpose(0, 2, 1, 3)
    return o_n.reshape(Bd, T, NSA_HEADS, HEAD_DIM), o_d


def _feature_major(cache):
    return cache.transpose(0, 2, 3, 1).reshape(cache.shape[0], -1, cache.shape[1])


def _chunk_hidden_body(x_ref, w_ref, o_ref, rows_sc, *, pages):
    width = x_ref.shape[1]
    page = x_ref.shape[2]
    chunks = pages * (page // CMP_STRIDE)
    for p in range(pages):
        rows_sc[p * page:(p + 1) * page, :] = x_ref[p].T
    acc = jnp.zeros(o_ref.shape, F32)
    for t in range(CMP_STRIDE):
        lhs = rows_sc[pl.ds(t, chunks, stride=CMP_STRIDE), :].astype(BF16)
        acc = acc + jnp.dot(lhs, w_ref[t], preferred_element_type=F32)
    o_ref[...] = acc


def _chunk_hidden(cache_t, w1, *, pages=32):
    P, width, page = cache_t.shape
    dh = HEAD_DIM
    G = width // dh
    H = w1.shape[1]
    half = CMP_STRIDE * dh
    w_ab = jnp.concatenate([w1[:half].reshape(CMP_STRIDE, dh, H), w1[half:].reshape(CMP_STRIDE, dh, H)], axis=-1)
    eye = jnp.eye(G, dtype=w1.dtype)
    w_rows = (w_ab[:, None, :, None, :] * eye[None, :, None, :, None]).reshape(CMP_STRIDE, width, G * 2 * H)
    pages = math.gcd(P, pages)
    cpp = page // CMP_STRIDE
    return pl.pallas_call(
        functools.partial(_chunk_hidden_body, pages=pages),
        grid=(P // pages,),
        in_specs=[pl.BlockSpec((pages, width, page), lambda i: (i, 0, 0)),
                  pl.BlockSpec(w_rows.shape, lambda i: (0, 0, 0))],
        out_specs=pl.BlockSpec((pages * cpp, G * 2 * H), lambda i: (i, 0)),
        out_shape=jax.ShapeDtypeStruct((P * cpp, G * 2 * H), F32),
        scratch_shapes=[pltpu.VMEM((pages * page, width), F32)],
        compiler_params=pltpu.CompilerParams(dimension_semantics=("parallel",)),
        name="chunk_hidden",
    )(cache_t, w_rows.astype(BF16))


def _page_specs(block, pages_per_step):
    def index(b, j, pt_ref, *, s):
        return (pt_ref[b, j * pages_per_step + s],) + (0,) * len(block)
    return [pl.BlockSpec((1,) + block, functools.partial(index, s=s)) for s in range(pages_per_step)]


def _seq_spec(block):
    return pl.BlockSpec((1,) + block, lambda b, j, pt_ref: (b,) + (0,) * len(block))


def _const_spec(block):
    return pl.BlockSpec(block, lambda b, j, pt_ref: (0,) * len(block))


def _sample_cmp_body(pt_ref, *refs, pps, n_steps, n_real, n_blocks, n_sel):
    pqk, pqv = refs[:pps], refs[pps:2 * pps]
    bias_k, bias_v, w2k, w2v, q_ref, seg_ref, o_ref, sel_ref, pqk_sc, pqv_sc = refs[2 * pps:]
    j = pl.program_id(1)
    cpp = pqk[0].shape[1]
    for s in range(pps):
        row0 = pl.multiple_of((j * pps + s) * cpp, cpp)
        pqk_sc[pl.ds(row0, cpp), :] = pqk[s][0]
        pqv_sc[pl.ds(row0, cpp), :] = pqv[s][0]

    @pl.when(j == n_steps - 1)
    def _():
        n_rows = pqk_sc.shape[0]
        H = w2k.shape[0]
        G = pqk_sc.shape[1] // (2 * H)
        RT = q_ref.shape[1] // G
        T = sel_ref.shape[1] // G
        visible = lax.broadcasted_iota(jnp.int32, (1, n_rows), 1) < n_real
        q_pos = PAST_LEN + lax.broadcasted_iota(jnp.int32, (T, 1), 0)

        def compress(sc, bias_ref, w2_ref, g):
            first = sc[:, g * 2 * H:g * 2 * H + H]
            second = pltpu.roll(sc[:, g * 2 * H + H:(g + 1) * 2 * H], n_rows - 1, 0)
            hid = _gelu_exact(first + second + bias_ref[...])
            return jnp.dot(hid.astype(BF16), w2_ref[...], preferred_element_type=F32).astype(BF16)

        for g in range(G):
            kc = compress(pqk_sc, bias_k, w2k, g)
            vc = compress(pqv_sc, bias_v, w2v, g)
            s = lax.dot_general(q_ref[0, g * RT:(g + 1) * RT, :], kc, NT_DIMS, preferred_element_type=F32)
            s = jnp.where(visible, s, NEG)
            m = jnp.max(s, axis=-1, keepdims=True)
            p = jnp.where(visible, jnp.exp(s - m), 0.0)
            p = p / jnp.sum(p, axis=-1, keepdims=True)
            o_ref[0, g * RT:(g + 1) * RT, :] = jnp.dot(p.astype(BF16), vc, preferred_element_type=F32)
            imp = jnp.sum(p.reshape(RT // T, T, n_rows), axis=0)
            imp_blk = jnp.dot(imp, seg_ref[...], preferred_element_type=F32, precision=lax.Precision.HIGHEST)
            sel_ref[0, g * T:(g + 1) * T, :] = _select_blocks(imp_blk, q_pos, n_blocks, n_sel).astype(sel_ref.dtype)


def _sample_cmp_select(page_table, pq_k, pq_v, bias_k, bias_v, w2k, w2v, q_rows, ns, nc, *, pps=16):
    Bd, n_pages = page_table.shape
    cpp = PAGE_SIZE // CMP_STRIDE
    n_rows = n_pages * cpp
    GRT, dh = q_rows.shape[1:]
    G = NSA_KV_HEADS
    T = GRT // (G * GQA_REP)
    H = w2k.shape[0]
    width = pq_k.shape[1]
    ns_pad = -(-ns // 128) * 128
    pps = min(pps, n_pages)
    n_steps = n_pages // pps
    pq_k = pq_k.reshape(-1, cpp, width)
    pq_v = pq_v.reshape(-1, cpp, width)
    body = functools.partial(_sample_cmp_body, pps=pps, n_steps=n_steps, n_real=nc, n_blocks=ns,
                             n_sel=min(SEL_TOPK, ns))
    grid_spec = pltpu.PrefetchScalarGridSpec(
        num_scalar_prefetch=1,
        grid=(Bd, n_steps),
        in_specs=(_page_specs((cpp, width), pps) + _page_specs((cpp, width), pps)
                  + [_const_spec((1, H)), _const_spec((1, H)), _const_spec((H, dh)), _const_spec((H, dh)),
                     _seq_spec((GRT, dh)), _const_spec((n_rows, ns_pad))]),
        out_specs=[_seq_spec((GRT, dh)), _seq_spec((G * T, ns_pad))],
        scratch_shapes=[pltpu.VMEM((n_rows, width), F32), pltpu.VMEM((n_rows, width), F32)],
    )
    return pl.pallas_call(
        body,
        grid_spec=grid_spec,
        out_shape=[jax.ShapeDtypeStruct((Bd, GRT, dh), F32), jax.ShapeDtypeStruct((Bd, G * T, ns_pad), BF16)],
        compiler_params=pltpu.CompilerParams(dimension_semantics=("parallel", "arbitrary")),
        name="sample_cmp_select",
    )(page_table, *([pq_k] * pps), *([pq_v] * pps), bias_k, bias_v, w2k, w2v, q_rows,
      _segment_matrix(n_rows, ns_pad))


def _sample_attend_body(pt_ref, *refs, pps, n_steps, n_new, w_buf):
    dk, dv, ksr, vsr = (refs[i * pps:(i + 1) * pps] for i in range(4))
    (qd_ref, qs_ref, sel_ref, exp_ref, dkn_ref, dvn_ref, ksn_ref, vsn_ref, kwn_ref, vwn_ref, wk_ref, wv_ref,
     od_ref, os_ref, ow_ref, md, ld, ad, ms, ls, as_) = refs[4 * pps:]
    j = pl.program_id(1)
    page = PAGE_SIZE

    @pl.when(j == 0)
    def _():
        for m_sc, l_sc, a_sc in ((md, ld, ad), (ms, ls, as_)):
            m_sc[...] = jnp.full(m_sc.shape, NEG, F32)
            l_sc[...] = jnp.zeros(l_sc.shape, F32)
            a_sc[...] = jnp.zeros(a_sc.shape, F32)

    qd = qd_ref[0]
    qs = qs_ref[0]
    sel = sel_ref[0]
    n_rows = qd.shape[0]
    heads = dk[0].shape[1] // page
    bf = lambda r: r[0].astype(BF16)

    def own_head(width):
        r = lax.broadcasted_iota(jnp.int32, (n_rows, width), 0) // (n_rows // heads)
        c = lax.broadcasted_iota(jnp.int32, (n_rows, width), 1) % heads
        return r == c

    sd = jnp.concatenate([lax.dot_general(qd, bf(r), NT_DIMS, preferred_element_type=F32) for r in dk], axis=1)
    _online_update(sd, own_head(sd.shape[1]), jnp.concatenate([bf(r) for r in dv], axis=0), md, ld, ad)
    ss = jnp.dot(qs, jnp.concatenate([bf(r) for r in ksr], axis=1), preferred_element_type=F32)
    col0 = pl.multiple_of(j * (pps * page), pps * page)
    picked = jnp.dot(sel, exp_ref[:, pl.ds(col0, pps * page)], preferred_element_type=F32) > 0.5
    _online_update(ss, picked, jnp.concatenate([bf(r) for r in vsr], axis=1), ms, ls, as_, v_t=True)

    @pl.when(j == n_steps - 1)
    def _():
        t_idx = lax.broadcasted_iota(jnp.int32, (n_rows, page), 0) % n_new
        k_idx = lax.broadcasted_iota(jnp.int32, (n_rows, page), 1)
        causal = k_idx <= t_idx
        new_d = lax.dot_general(qd, dkn_ref[0], NT_DIMS, preferred_element_type=F32)
        _online_update(new_d, own_head(page) & ((k_idx // heads) <= t_idx), dvn_ref[0], md, ld, ad)
        od_ref[0] = ad[...] / ld[...]
        picked_new = (jnp.dot(sel, exp_ref[:, pl.ds(n_steps * pps * page, page)],
                              preferred_element_type=F32) > 0.5) & causal
        _online_update(jnp.dot(qs, ksn_ref[0], preferred_element_type=F32), picked_new, vsn_ref[0],
                       ms, ls, as_, v_t=True)
        os_ref[0] = as_[...] / ls[...]
        tb = lax.broadcasted_iota(jnp.int32, (n_rows, w_buf), 0) % n_new
        kb = lax.broadcasted_iota(jnp.int32, (n_rows, w_buf), 1)
        mask = jnp.concatenate([(w_buf + tb - kb) < WINDOW, causal], axis=1)
        k_all = jnp.concatenate([bf(wk_ref), kwn_ref[0]], axis=1)
        v_all = jnp.concatenate([bf(wv_ref), vwn_ref[0]], axis=1)
        s = jnp.where(mask, jnp.dot(qs, k_all, preferred_element_type=F32), NEG)
        m = jnp.max(s, axis=-1, keepdims=True)
        p = jnp.where(mask, jnp.exp(s - m), 0.0)
        pv = lax.dot_general(p.astype(BF16), v_all, NT_DIMS, preferred_element_type=F32)
        ow_ref[0] = pv / jnp.sum(p, axis=-1, keepdims=True)


def _sample_attend(page_table, c_dk, c_dv, c_ks, c_vs, qd_blk, qs_blk, sel_rows, dk_new, dv_new, ks_new, vs_new,
                   kw_new, vw_new, win_k, win_v, *, pps=16):
    Bd, n_pages = page_table.shape
    pps = min(pps, n_pages)
    n_steps = n_pages // pps
    w_buf = win_k.shape[2]
    dw, sw = c_dk.shape[2], c_ks.shape[1]
    drows = c_dk.shape[1]
    rd, rs = qd_blk.shape[1], qs_blk.shape[1]
    assert rd == rs
    ns_pad = sel_rows.shape[2]
    n_keys = (n_pages + 1) * PAGE_SIZE
    n_new = rs // NSA_HEADS
    expand = (jnp.arange(n_keys)[None, :] // SEL_LEN == jnp.arange(ns_pad)[:, None]).astype(BF16)
    body = functools.partial(_sample_attend_body, pps=pps, n_steps=n_steps, n_new=n_new, w_buf=w_buf)
    page = PAGE_SIZE
    grid_spec = pltpu.PrefetchScalarGridSpec(
        num_scalar_prefetch=1,
        grid=(Bd, n_steps),
        in_specs=(_page_specs((drows, dw), pps) + _page_specs((drows, dw), pps)
                  + _page_specs((sw, page), pps) + _page_specs((sw, page), pps)
                  + [_seq_spec((rd, dw)), _seq_spec((rs, sw)), _seq_spec((rs, ns_pad)),
                     _const_spec((ns_pad, n_keys)),
                     _seq_spec((page, dw)), _seq_spec((page, dw)), _seq_spec((sw, page)), _seq_spec((sw, page)),
                     _seq_spec((sw, page)), _seq_spec((sw, page)), _seq_spec((sw, w_buf)), _seq_spec((sw, w_buf))]),
        out_specs=[_seq_spec((rd, dw)), _seq_spec((rs, sw)), _seq_spec((rs, sw))],
        scratch_shapes=[pltpu.VMEM((rd, 1), F32), pltpu.VMEM((rd, 1), F32), pltpu.VMEM((rd, dw), F32),
                        pltpu.VMEM((rs, 1), F32), pltpu.VMEM((rs, 1), F32), pltpu.VMEM((rs, sw), F32)],
    )
    return pl.pallas_call(
        body,
        grid_spec=grid_spec,
        out_shape=[jax.ShapeDtypeStruct((Bd, rd, dw), F32), jax.ShapeDtypeStruct((Bd, rs, sw), F32),
                   jax.ShapeDtypeStruct((Bd, rs, sw), F32)],
        compiler_params=pltpu.CompilerParams(dimension_semantics=("parallel", "arbitrary"),
                                             vmem_limit_bytes=(V7X_VMEM_BYTES * 3) // 4),
        name="sample_attend",
    )(page_table, *([c_dk] * pps), *([c_dv] * pps), *([c_ks] * pps), *([c_vs] * pps),
      qd_blk, qs_blk, sel_rows, expand, dk_new, dv_new, ks_new, vs_new, kw_new, vw_new, win_k, win_v)


def mix_out(o_n, o_d, g_nsa, g_sub, w_out, lam_init):
    B, T = o_n.shape[:2]
    a = rms_norm(o_n, g_nsa.reshape(NSA_HEADS, HEAD_DIM))
    d = rms_norm(o_d, g_sub) * (1.0 - lam_init)
    return jnp.concatenate([a.reshape(B, T, NSA_Q), d.reshape(B, T, DIFF_W)], axis=-1) @ w_out


def mem_kv(mem, g_src, w_mk, w_mv):
    B, M, _ = mem.shape
    m = rms_norm(mem, g_src)
    return ((m @ w_mk).reshape(B, M, MEM_HEADS, MEM_HEAD_DIM),
            (m @ w_mv).reshape(B, M, MEM_HEADS, MEM_HEAD_DIM))


def mem_block(h, g, w_mq, mk, mv, w_mo):
    B, T, _ = h.shape
    q = (rms_norm(h, g) @ w_mq).reshape(B, T, MEM_HEADS, MEM_HEAD_DIM)
    s = jnp.einsum('bqhd,bmhd->bhqm', q, mk) * (MEM_HEAD_DIM ** -0.5)
    p = jax.nn.softmax(s.astype(jnp.float32), axis=-1)
    o = jnp.einsum('bhqm,bmhd->bqhd', p.astype(mv.dtype), mv)
    return h + o.reshape(B, T, MEM_W) @ w_mo


def peer_ffn(x, w_q, sub_keys, u, v):
    n = x.shape[0]
    pad = (-n) % PEER_CHUNK
    xp = jnp.pad(x, ((0, pad), (0, 0))).reshape(-1, PEER_CHUNK, D_MODEL)

    def chunk(xc):
        q = (xc @ w_q).reshape(PEER_CHUNK, PEER_HEADS, 2, PEER_DK // 2)
        s1 = jnp.einsum('thd,kd->thk', q[:, :, 0], sub_keys[0]).astype(jnp.float32)
        s2 = jnp.einsum('thd,kd->thk', q[:, :, 1], sub_keys[1]).astype(jnp.float32)
        v1, i1 = lax.top_k(s1, PEER_TOPK)
        v2, i2 = lax.top_k(s2, PEER_TOPK)
        cand = (v1[..., :, None] + v2[..., None, :]).reshape(PEER_CHUNK, PEER_HEADS, PEER_TOPK * PEER_TOPK)
        cidx = (i1[..., :, None] * PEER_KEYS + i2[..., None, :]).reshape(PEER_CHUNK, PEER_HEADS, PEER_TOPK * PEER_TOPK)
        sc, sel = lax.top_k(cand, PEER_TOPK)
        eidx = jnp.take_along_axis(cidx, sel, axis=-1)
        g = jax.nn.softmax(sc, axis=-1)
        ue = u[eidx]
        ve = v[eidx]
        act = jax.nn.gelu(jnp.einsum('td,thkd->thk', xc, ue).astype(jnp.float32), approximate=False)
        return jnp.einsum('thk,thkd->td', (g * act).astype(xc.dtype), ve)

    return lax.map(chunk, xp).reshape(-1, D_MODEL)[:n]


def ffn_block(h, g, w_q, sub_keys, u, v):
    B, T, D = h.shape
    return h + peer_ffn(rms_norm(h, g).reshape(B * T, D), w_q, sub_keys, u, v).reshape(B, T, D)


LOWEST = -3.0e38
PEER_PAIRS = PEER_HEADS * PEER_TOPK
V7X_VMEM_BYTES = 64 * 1024 * 1024
SUBLANES = 8


LANES = 128


def _take_max(work, ids, n_ids):
    mx = jnp.max(work, axis=0, keepdims=True)
    first = jnp.min(jnp.where(work == mx, ids, n_ids), axis=0, keepdims=True)
    return mx, first, jnp.where(ids == first, LOWEST, work)


def _top_sorted(scores, k):
    n = scores.shape[0]
    ids = lax.broadcasted_iota(jnp.int32, scores.shape, 0)
    vals, idxs = [], []
    for _ in range(k):
        mx, first, scores = _take_max(scores, ids, n)
        vals.append(mx)
        idxs.append(first)
    return jnp.concatenate(vals, axis=0), jnp.concatenate(idxs, axis=0)


def _peer_topk_body(s1_ref, s2_ref, e_ref, g_ref):
    K, NK = PEER_TOPK, PEER_KEYS
    width = s1_ref.shape[1]

    def lanes(i, carry):
        c0 = pl.multiple_of(i * LANES, LANES)
        v1, i1 = _top_sorted(s1_ref[:, pl.ds(c0, LANES)], K)
        v2, i2 = _top_sorted(s2_ref[:, pl.ds(c0, LANES)], K)
        cands, cidxs, cids = [], [], []
        for r in range(K):
            rows = min(K, -(-(K // (r + 1)) // SUBLANES) * SUBLANES)
            c = lax.broadcasted_iota(jnp.int32, (rows, LANES), 0)
            live = c < K // (r + 1)
            cands.append(jnp.where(live, v1[r:r + 1] + v2[:rows], LOWEST))
            cidxs.append(i1[r:r + 1] * NK + i2[:rows])
            cids.append(jnp.where(live, c + r * K, K * K))
        cand = jnp.concatenate(cands, axis=0)
        cidx = jnp.concatenate(cidxs, axis=0)
        cid = jnp.concatenate(cids, axis=0)
        vals, experts = [], []
        for _ in range(K):
            mc, fc, cand = _take_max(cand, cid, K * K)
            vals.append(mc)
            experts.append(jnp.sum(jnp.where(cid == fc, cidx, 0), axis=0, keepdims=True))
        sc = jnp.concatenate(vals, axis=0)
        p = jnp.exp(sc - jnp.max(sc, axis=0, keepdims=True))
        g_ref[:, pl.ds(c0, LANES)] = p / jnp.sum(p, axis=0, keepdims=True)
        e_ref[:, pl.ds(c0, LANES)] = jnp.concatenate(experts, axis=0)
        return carry

    lax.fori_loop(0, width // LANES, lanes, 0)


def peer_topk(s1t, s2t, *, tm=1024):
    M = s1t.shape[1]
    tm = math.gcd(M, tm)
    spec_in = pl.BlockSpec((PEER_KEYS, tm), lambda i: (0, i))
    spec_out = pl.BlockSpec((PEER_TOPK, tm), lambda i: (0, i))
    return pl.pallas_call(
        _peer_topk_body,
        grid=(M // tm,),
        in_specs=[spec_in, spec_in],
        out_specs=[spec_out, spec_out],
        out_shape=[jax.ShapeDtypeStruct((PEER_TOPK, M), jnp.int32),
                   jax.ShapeDtypeStruct((PEER_TOPK, M), F32)],
        compiler_params=pltpu.CompilerParams(dimension_semantics=("parallel",)),
        name="peer_topk",
    )(s1t, s2t)


def _peer_route_body(e_ref, g_ref, o_ref):
    tt = e_ref.shape[0]
    NK = PEER_KEYS
    sub = lax.broadcasted_iota(jnp.int32, (NK, PEER_PAIRS), 0)

    def one(t):
        e = e_ref[pl.ds(t, 1), :]
        g = g_ref[pl.ds(t, 1), :]
        rows = jnp.where((e // NK) == sub, g, 0.0).astype(BF16)
        cols = jnp.where((e % NK) == sub, 1.0, 0.0).astype(BF16)
        return lax.dot_general(rows, cols, (((1,), (1,)), ((), ())), preferred_element_type=F32)

    def group(i, carry):
        t0 = pl.multiple_of(i * SUBLANES, SUBLANES)
        g8 = jnp.stack([one(t0 + s) for s in range(SUBLANES)], axis=0)
        by_row = jnp.swapaxes(g8, 0, 1)
        for i1 in range(NK):
            o_ref[pl.ds(t0, SUBLANES), pl.ds(i1 * NK, NK)] = by_row[i1].astype(o_ref.dtype)
        return carry

    lax.fori_loop(0, tt // SUBLANES, group, 0)


def peer_route(eidx, gate, *, tt=256):
    N = eidx.shape[0]
    spec_in = pl.BlockSpec((tt, PEER_PAIRS), lambda i: (i, 0))
    return pl.pallas_call(
        _peer_route_body,
        grid=(N // tt,),
        in_specs=[spec_in, spec_in],
        out_specs=pl.BlockSpec((tt, PEER_EXPERTS), lambda i: (i, 0)),
        out_shape=jax.ShapeDtypeStruct((N, PEER_EXPERTS), BF16),
        compiler_params=pltpu.CompilerParams(dimension_semantics=("parallel",)),
        name="peer_route",
    )(eidx, gate)


def _gelu_exact(x):
    return 0.5 * x * (1.0 + lax.erf(x * (2.0 ** -0.5)))


def _peer_experts_body(x_ref, ut_ref, v_ref, g_ref, o_ref):
    c = pl.program_id(1)

    @pl.when(c == 0)
    def _():
        o_ref[...] = jnp.zeros(o_ref.shape, F32)

    h = jnp.dot(x_ref[...], ut_ref[...], preferred_element_type=F32)
    a = (g_ref[...].astype(F32) * _gelu_exact(h)).astype(BF16)
    o_ref[...] += jnp.dot(a, v_ref[...], preferred_element_type=F32)


def peer_experts(x, ut, v, gmat, *, tt=1024, ce=1024):
    N, D = x.shape
    E = ut.shape[1]
    return pl.pallas_call(
        _peer_experts_body,
        grid=(N // tt, E // ce),
        in_specs=[
            pl.BlockSpec((tt, D), lambda i, c: (i, 0)),
            pl.BlockSpec((D, ce), lambda i, c: (0, c)),
            pl.BlockSpec((ce, D), lambda i, c: (c, 0)),
            pl.BlockSpec((tt, ce), lambda i, c: (i, c)),
        ],
        out_specs=pl.BlockSpec((tt, D), lambda i, c: (i, 0)),
        out_shape=jax.ShapeDtypeStruct((N, D), F32),
        compiler_params=pltpu.CompilerParams(
            dimension_semantics=("parallel", "arbitrary"),
            vmem_limit_bytes=(V7X_VMEM_BYTES * 3) // 4),
        name="peer_experts",
    )(x, ut, v, gmat)


def peer_ffn_dense(x, w_q, sub_keys, u, v):
    n = x.shape[0]
    q = (x @ w_q).reshape(n, PEER_HEADS, 2, PEER_DK // 2)
    s1t = jnp.einsum('kd,thd->kth', sub_keys[0], q[:, :, 0]).astype(F32).reshape(PEER_KEYS, n * PEER_HEADS)
    s2t = jnp.einsum('kd,thd->kth', sub_keys[1], q[:, :, 1]).astype(F32).reshape(PEER_KEYS, n * PEER_HEADS)
    eidx_t, gate_t = peer_topk(s1t, s2t)
    pairs = lambda t: t.reshape(PEER_TOPK, n, PEER_HEADS).transpose(1, 2, 0).reshape(n, PEER_PAIRS)
    gmat = peer_route(pairs(eidx_t), pairs(gate_t))
    return peer_experts(x.astype(BF16), u.astype(BF16).T, v.astype(BF16), gmat)


def _final_norm_body(x_ref, g_ref, o_ref):
    x = x_ref[...]
    y = x * lax.rsqrt(jnp.mean(x * x, axis=-1, keepdims=True) + RMS_EPS)
    o_ref[...] = y * g_ref[...]


def _final_norm(h, g):
    B, T, D = h.shape
    x = h.reshape(B * T, D)
    n = x.shape[0]
    tm = 512
    out = pl.pallas_call(
        _final_norm_body,
        grid=(n // tm,),
        in_specs=[pl.BlockSpec((tm, D), lambda i: (i, 0)), pl.BlockSpec((1, D), lambda i: (0, 0))],
        out_specs=pl.BlockSpec((tm, D), lambda i: (i, 0)),
        out_shape=jax.ShapeDtypeStruct((n, D), jnp.float32),
    )(x, g.reshape(1, D))
    return out.reshape(B, T, D)


def kernel(x_prompt, x_sample, cache_cmp_k, cache_cmp_v, cache_slc_k, cache_slc_v,
           cache_diff_k, cache_diff_v, cache_win_k, cache_win_v, cache_mem_k, cache_mem_v,
           page_table, mem_prompt, norm_mix, w_in, cmp_pe_k, cmp_pe_v, cmp_k_w1, cmp_k_w2,
           cmp_v_w1, cmp_v_w2, nsa_out_norm, diff_lq1, diff_lk1, diff_lq2, diff_lk2, diff_subln,
           w_out, norm_mem_q, norm_mem_src, w_mq, w_mk, w_mv, w_mo, norm_ffn, peer_wq, peer_keys,
           peer_u, peer_v, norm_final):
    f32 = jnp.float32
    hp, hs = x_prompt, x_sample
    pos_p = jnp.arange(x_prompt.shape[1])
    pos_s = PAST_LEN + jnp.arange(x_sample.shape[1])
    l = 0
    lam_init = 0.8 - 0.6 * math.exp(-0.3 * l)
    lam = (jnp.exp(jnp.sum(diff_lq1[l].astype(f32) * diff_lk1[l].astype(f32)))
           - jnp.exp(jnp.sum(diff_lq2[l].astype(f32) * diff_lk2[l].astype(f32))) + lam_init)
    cmp_w = (cmp_pe_k[l], cmp_k_w1[l], cmp_k_w2[l], cmp_pe_v[l], cmp_v_w1[l], cmp_v_w2[l])

    pp = project_in(rms_norm(hp, norm_mix[l]), w_in[l], pos_p)
    o_n, o_d = prompt_mixer(pp, lam, cmp_w)
    hp = hp + mix_out(o_n, o_d, nsa_out_norm[l], diff_subln[l], w_out[l], lam_init)
    mk, mv = mem_kv(mem_prompt, norm_mem_src[l], w_mk[l], w_mv[l])
    hp = mem_block(hp, norm_mem_q[l], w_mq[l], mk, mv, w_mo[l])
    Bp, Tp = x_prompt.shape[:2]
    w_p = min(WINDOW, Tp)

    ps = project_in(rms_norm(hs, norm_mix[l]), w_in[l], pos_s)
    o_n, o_d = sample_mixer(ps, page_table, cache_cmp_k[l], cache_cmp_v[l], cache_slc_k[l],
                            cache_slc_v[l], cache_diff_k[l], cache_diff_v[l], cache_win_k[l],
                            cache_win_v[l], lam, cmp_w)
    hs = hs + mix_out(o_n, o_d, nsa_out_norm[l], diff_subln[l], w_out[l], lam_init)
    hs = mem_block(hs, norm_mem_q[l], w_mq[l], cache_mem_k[l], cache_mem_v[l], w_mo[l])
    Bs, Ts = x_sample.shape[:2]

    h_all = jnp.concatenate([hp.reshape(Bp * Tp, D_MODEL), hs.reshape(Bs * Ts, D_MODEL)], axis=0)
    h_all = h_all + peer_ffn_dense(rms_norm(h_all, norm_ffn[l]), peer_wq[l], peer_keys[l],
                                   peer_u[l], peer_v[l])
    y_all = _final_norm(h_all[None], norm_final)[0]
    y_prompt = y_all[:Bp * Tp].reshape(Bp, Tp, D_MODEL)
    y_sample = y_all[Bp * Tp:].reshape(Bs, Ts, D_MODEL)
    st = lambda t: t[None]
    return (y_prompt, y_sample,
            st(pp[2]), st(pp[3]), st(pp[4]), st(pp[5]), st(pp[6][:, Tp - w_p:]), st(pp[7][:, Tp - w_p:]),
            st(pp[10].reshape(Bp, Tp, DIFF_HEADS, 2 * DIFF_DIM)), st(pp[11]), st(mk), st(mv),
            st(ps[2]), st(ps[3]), st(ps[4]), st(ps[5]),
            st(jnp.concatenate([cache_win_k[l], ps[6]], axis=1)[:, Ts:]),
            st(jnp.concatenate([cache_win_v[l], ps[7]], axis=1)[:, Ts:]),
            st(ps[10].reshape(Bs, Ts, DIFF_HEADS, 2 * DIFF_DIM)), st(ps[11]))
```
